```python
import math
import jax, jax.numpy as jnp
from jax import lax
import numpy as np

D_MODEL = 2048
BATCH = 8
SEQ = 2048
DEPTH = 4

D_FF = 5632
A_DILATION_PAIRS = ((128, 1), (512, 4), (2048, 16))
A_HEADS_PER_GROUP = 4
A_HEADS = A_HEADS_PER_GROUP * len(A_DILATION_PAIRS)
A_HEAD_DIM = 128
A_OUT = A_HEADS_PER_GROUP * A_HEAD_DIM
ALIBI_MAX_BIAS = 8.0
B_HEADS = 8
B_HEAD_DIM = 128
B_WIDTH = B_HEADS * B_HEAD_DIM
B_CONV = 4
B_CHUNK = 64
C_HEAD_DIM = 64
C_HEADS = 16
C_WIDTH = C_HEADS * C_HEAD_DIM
C_DECAY_LORA = 96
C_ICLR_LORA = 96
C_GATE_LORA = 256
C_GN_EPS = 64e-5
A_COLS = 3 * A_HEADS * A_HEAD_DIM
B_COLS = 4 * B_WIDTH + 2 * B_HEADS
C_COLS = 3 * C_WIDTH + C_DECAY_LORA + C_ICLR_LORA + C_GATE_LORA
G_COLS = 3 * D_MODEL
IN_COLS = A_COLS + B_COLS + C_COLS + G_COLS

NORM_EPS = 1e-6
NEG_INF = -1e30

kernel_name = 'hybrid_dilated_gdn_rwkv7_macaron'


def split_cols(t, widths):
    return jnp.split(t, np.cumsum(widths)[:-1].tolist(), axis=-1)


def rms_norm(x, g, eps=NORM_EPS):
    xf = x.astype(jnp.float32)
    y = xf * lax.rsqrt(jnp.mean(xf * xf, axis=-1, keepdims=True) + eps)
    return (y * g.astype(jnp.float32)).astype(x.dtype)


def l2_normalize(x, eps=1e-6):
    xf = x.astype(jnp.float32)
    return xf * lax.rsqrt(jnp.sum(xf * xf, axis=-1, keepdims=True) + eps)


def swiglu(h, w_gu, w_down):
    gate, up = jnp.split(h @ w_gu, 2, axis=-1)
    return (jax.nn.silu(gate) * up) @ w_down


def to_heads(t, n_heads):
    return t.reshape(*t.shape[:-1], n_heads, -1)


def token_shift(x, mu):
    prev = jnp.pad(x, ((0, 0), (1, 0), (0, 0)))[:, :-1]
    return x + (prev - x) * mu


def causal_depthwise_conv(x, w):
    K, S = w.shape[0], x.shape[1]
    xp = jnp.pad(x, ((0, 0), (K - 1, 0), (0, 0)))
    return sum(xp[:, j:j + S] * w[j] for j in range(K))


def banded_causal_attention(q, k, v, slopes, steps, dilation):
    N, H, L, Dh = q.shape
    blk = steps
    nb = -(-L // blk)
    lp = nb * blk
    pad = ((0, 0), (0, 0), (0, lp - L), (0, 0))
    q, k, v = (jnp.pad(t, pad) for t in (q, k, v))

    def band(t):
        prev = jnp.pad(t, ((0, 0), (0, 0), (blk, 0), (0, 0)))[:, :, :lp].reshape(N, H, nb, blk, Dh)
        return jnp.concatenate([prev, t.reshape(N, H, nb, blk, Dh)], axis=3)

    s = jnp.einsum('nhbqd,nhbkd->nhbqk', q.reshape(N, H, nb, blk, Dh), band(k)).astype(jnp.float32) * (Dh ** -0.5)
    qi = jnp.arange(blk)[:, None]
    ki = jnp.arange(2 * blk)[None, :]
    delta = qi + blk - ki
    key_pos = jnp.arange(nb)[:, None, None] * blk - blk + ki[None]
    valid = (delta >= 0) & (delta <= steps) & (key_pos >= 0)
    bias = -slopes[:, None, None, None] * (delta * dilation).astype(jnp.float32)
    s = jnp.where(valid, s + bias, NEG_INF)
    lse = jax.nn.logsumexp(s, axis=-1)
    p = jnp.exp(s - lse[..., None])
    o = jnp.einsum('nhbqk,nhbkd->nhbqd', p.astype(v.dtype), band(v))
    return o.reshape(N, H, lp, Dh)[:, :, :L], lse.reshape(N, H, lp)[:, :, :L]


def mixer_dilated_attention(q, k, v):
    Bn, S = q.shape[0], q.shape[1]
    hpg, dh = A_HEADS_PER_GROUP, A_HEAD_DIM
    slopes = 2.0 ** (-ALIBI_MAX_BIAS * (jnp.arange(A_HEADS, dtype=jnp.float32) + 1.0) / A_HEADS)
    outs, lses = [], []
    for gi, (win, dil) in enumerate(A_DILATION_PAIRS):
        hs = slice(gi * hpg, (gi + 1) * hpg)
        L = S // dil

        def to_residue(t):
            t = t[:, :, hs].reshape(Bn, L, dil, hpg, dh).transpose(0, 2, 3, 1, 4)
            return t.reshape(Bn * dil, hpg, L, dh)

        o, lse = banded_causal_attention(to_residue(q), to_residue(k), to_residue(v), slopes[hs], win // dil, dil)
        outs.append(o.reshape(Bn, dil, hpg, L, dh).transpose(0, 3, 1, 2, 4).reshape(Bn, S, hpg, dh))
        lses.append(lse.reshape(Bn, dil, hpg, L).transpose(0, 3, 1, 2).reshape(Bn, S, hpg))
    wts = jax.nn.softmax(jnp.stack(lses, axis=2), axis=2)
    o = jnp.einsum('bsgh,bsghd->bshd', wts.astype(q.dtype), jnp.stack(outs, axis=2))
    return o.reshape(Bn, S, A_OUT)


def chunked_gated_delta_rule(q, k, v, log_decay, beta):
    Bn, H, S, Dk = q.shape
    Dv = v.shape[-1]
    C = B_CHUNK
    n = S // C
    q, k, v = (t.reshape(Bn, H, n, C, t.shape[-1]) for t in (q, k, v))
    beta = beta.reshape(Bn, H, n, C)
    g = jnp.cumsum(log_decay.reshape(Bn, H, n, C), axis=-1)
    tril = jnp.tril(jnp.ones((C, C), dtype=bool))
    tril_strict = jnp.tril(jnp.ones((C, C), dtype=bool), -1)
    decay = jnp.exp(jnp.where(tril, g[..., :, None] - g[..., None, :], NEG_INF))
    kb = k * beta[..., None]
    vb = v * beta[..., None]
    m = jnp.where(tril_strict, jnp.einsum('bhncd,bhnjd->bhncj', kb, k) * decay, 0.0)
    eye = jnp.eye(C, dtype=jnp.float32)
    rhs = jnp.concatenate([vb, kb * jnp.exp(g)[..., None]], axis=-1)
    sol = lax.linalg.triangular_solve(eye + m, rhs, left_side=True, lower=True, unit_diagonal=True)
    u, w = sol[..., :Dv], sol[..., Dv:]
    att = jnp.where(tril, jnp.einsum('bhncd,bhnjd->bhncj', q, k) * decay, 0.0)
    q_dec = q * jnp.exp(g)[..., None]
    k_dec = k * jnp.exp(g[..., -1:] - g)[..., None]
    chunk_decay = jnp.exp(g[..., -1])

    def step(state, xs):
        q_c, k_c, u_c, w_c, att_c, cd = xs
        v_new = u_c - jnp.einsum('bhck,bhkv->bhcv', w_c, state)
        o_c = jnp.einsum('bhck,bhkv->bhcv', q_c, state) + jnp.einsum('bhcj,bhjv->bhcv', att_c, v_new)
        state = state * cd[..., None, None] + jnp.einsum('bhck,bhcv->bhkv', k_c, v_new)
        return state, o_c

    xs = tuple(jnp.moveaxis(t, 2, 0) for t in (q_dec, k_dec, u, w, att, chunk_decay))
    s0 = jnp.zeros((Bn, H, Dk, Dv), jnp.float32)
    _, o = lax.scan(step, s0, xs)
    return jnp.moveaxis(o, 0, 2).reshape(Bn, H, S, Dv)


def mixer_gated_deltanet(qkv, z, alpha, beta_logit, conv_w, a_log, dt_bias, norm_w):
    Bn, S = qkv.shape[0], qkv.shape[1]
    f32 = jnp.float32
    qkv = jax.nn.silu(causal_depthwise_conv(qkv, conv_w))
    q, k, v = jnp.split(qkv, 3, axis=-1)
    heads = lambda t: to_heads(t, B_HEADS).transpose(0, 2, 1, 3)
    q = l2_normalize(heads(q)) * (B_HEAD_DIM ** -0.5)
    k = l2_normalize(heads(k))
    v = heads(v).astype(f32)
    log_decay = (-jnp.exp(a_log.astype(f32)) * jax.nn.softplus(alpha.astype(f32) + dt_bias.astype(f32))).transpose(0, 2, 1)
    beta = jax.nn.sigmoid(beta_logit.astype(f32)).transpose(0, 2, 1)
    o = chunked_gated_delta_rule(q, k, v, log_decay, beta).transpose(0, 2, 1, 3)
    o = rms_norm(o, norm_w) * jax.nn.silu(to_heads(z, B_HEADS).astype(f32))
    return o.reshape(Bn, S, B_WIDTH).astype(z.dtype)


def rwkv7_recurrence(r, log_w, k, v, kk, a):
    xs = tuple(jnp.moveaxis(t.astype(jnp.float32), 1, 0) for t in (r, log_w, k, v, kk, a))
    Bn, H, N = r.shape[0], r.shape[2], r.shape[3]

    def step(state, inp):
        r_t, lw_t, k_t, v_t, kk_t, a_t = inp
        sa = jnp.einsum('bhvk,bhk->bhv', state, -kk_t)
        state = (state * jnp.exp(lw_t)[:, :, None, :]
                 + sa[..., None] * (kk_t * a_t)[:, :, None, :]
                 + v_t[..., None] * k_t[:, :, None, :])
        return state, jnp.einsum('bhvk,bhk->bhv', state, r_t)

    s0 = jnp.zeros((Bn, H, N, N), jnp.float32)
    _, o = lax.scan(step, s0, xs)
    return jnp.moveaxis(o, 0, 1)


def mixer_rwkv7(cols, mu, w0, w2, a0, a2, g2, k_k, k_a, r_k, gn_w, gn_b):
    Bn, S = cols.shape[0], cols.shape[1]
    cols = token_shift(cols, mu)
    r, k, v, xw, xa, xg = split_cols(cols, (C_WIDTH, C_WIDTH, C_WIDTH, C_DECAY_LORA, C_ICLR_LORA, C_GATE_LORA))
    w = -jax.nn.softplus(-(w0 + jnp.tanh(xw) @ w2)) - 0.5
    log_w = -jnp.exp(w.astype(jnp.float32))
    a = jax.nn.sigmoid(a0 + xa @ a2)
    g = jax.nn.sigmoid(xg) @ g2
    kk = l2_normalize(to_heads(k * k_k, C_HEADS))
    k = k * (1 + (a - 1) * k_a)
    rh, kh, vh = to_heads(r, C_HEADS), to_heads(k, C_HEADS), to_heads(v, C_HEADS)
    o = rwkv7_recurrence(rh, to_heads(log_w, C_HEADS), kh, vh, kk, to_heads(a, C_HEADS))
    mean = jnp.mean(o, axis=-1, keepdims=True)
    var = jnp.mean(jnp.square(o - mean), axis=-1, keepdims=True)
    o = ((o - mean) * lax.rsqrt(var + C_GN_EPS)).reshape(Bn, S, C_WIDTH) * gn_w + gn_b
    bonus = jnp.sum(rh * kh * r_k, axis=-1, keepdims=True) * vh
    o = (o + bonus.reshape(Bn, S, C_WIDTH)) * g
    return o.astype(cols.dtype)


def setup_inputs(seed: int = 0) -> dict:
    key = jax.random.key(seed)
    ks = iter(list(jax.random.split(key, 40)))
    nrm = lambda shape, scale: jax.random.normal(next(ks), shape, jnp.float32) * scale
    unif = lambda shape, lo, hi: jax.random.uniform(next(ks), shape, jnp.float32, lo, hi)
    gain = lambda shape: 1.0 + nrm(shape, 0.02)
    L = DEPTH
    x = nrm((BATCH, SEQ, D_MODEL), 1.0)
    ffn1_norm = gain((L, D_MODEL))
    ffn1_w_gu = nrm((L, D_MODEL, 2 * D_FF), D_MODEL ** -0.5)
    ffn1_w_down = nrm((L, D_FF, D_MODEL), D_FF ** -0.5)
    mix_norm = gain((L, D_MODEL))
    w_in = nrm((L, D_MODEL, IN_COLS), D_MODEL ** -0.5)
    b_conv = nrm((L, B_CONV, 3 * B_WIDTH), B_CONV ** -0.5)
    b_a_log = jnp.log(unif((L, B_HEADS), 1.0, 16.0))
    dt = jnp.exp(unif((L, B_HEADS), math.log(1e-3), math.log(1e-1)))
    b_dt_bias = dt + jnp.log(-jnp.expm1(-dt))
    b_norm = gain((L, B_HEAD_DIM))
    c_mu = unif((L, C_COLS), 0.0, 1.0)
    c_w0 = unif((L, C_WIDTH), -6.5, -1.5)
    c_w2 = nrm((L, C_DECAY_LORA, C_WIDTH), 0.1 * C_DECAY_LORA ** -0.5)
    c_a0 = nrm((L, C_WIDTH), 0.1)
    c_a2 = nrm((L, C_ICLR_LORA, C_WIDTH), 0.5 * C_ICLR_LORA ** -0.5)
    c_g2 = nrm((L, C_GATE_LORA, C_WIDTH), C_GATE_LORA ** -0.5)
    c_k_k = 0.85 + nrm((L, C_WIDTH), 0.05)
    c_k_a = 1.0 + nrm((L, C_WIDTH), 0.05)
    c_r_k = nrm((L, C_HEADS, C_HEAD_DIM), 0.1)
    c_gn_w = gain((L, C_WIDTH))
    c_gn_b = nrm((L, C_WIDTH), 0.01)
    proj_a = nrm((L, A_OUT, D_MODEL), A_OUT ** -0.5)
    proj_b = nrm((L, B_WIDTH, D_MODEL), B_WIDTH ** -0.5)
    proj_c = nrm((L, C_WIDTH, D_MODEL), C_WIDTH ** -0.5)
    w_out = nrm((L, D_MODEL, D_MODEL), D_MODEL ** -0.5)
    ffn2_norm = gain((L, D_MODEL))
    ffn2_w_gu = nrm((L, D_MODEL, 2 * D_FF), D_MODEL ** -0.5)
    ffn2_w_down = nrm((L, D_FF, D_MODEL), D_FF ** -0.5)
    final_norm = gain((D_MODEL,))
    return {'x': x, 'ffn1_norm': ffn1_norm, 'ffn1_w_gu': ffn1_w_gu, 'ffn1_w_down': ffn1_w_down,
            'mix_norm': mix_norm, 'w_in': w_in, 'b_conv': b_conv, 'b_a_log': b_a_log,
            'b_dt_bias': b_dt_bias, 'b_norm': b_norm, 'c_mu': c_mu, 'c_w0': c_w0, 'c_w2': c_w2,
            'c_a0': c_a0, 'c_a2': c_a2, 'c_g2': c_g2, 'c_k_k': c_k_k, 'c_k_a': c_k_a, 'c_r_k': c_r_k,
            'c_gn_w': c_gn_w, 'c_gn_b': c_gn_b, 'proj_a': proj_a, 'proj_b': proj_b, 'proj_c': proj_c,
            'w_out': w_out, 'ffn2_norm': ffn2_norm, 'ffn2_w_gu': ffn2_w_gu, 'ffn2_w_down': ffn2_w_down,
            'final_norm': final_norm}


def reference(x, ffn1_norm, ffn1_w_gu, ffn1_w_down, mix_norm, w_in, b_conv, b_a_log, b_dt_bias, b_norm,
              c_mu, c_w0, c_w2, c_a0, c_a2, c_g2, c_k_k, c_k_a, c_r_k, c_gn_w, c_gn_b,
              proj_a, proj_b, proj_c, w_out, ffn2_norm, ffn2_w_gu, ffn2_w_down, final_norm):
    Bn, S = x.shape[0], x.shape[1]
    for l in range(DEPTH):
        x = x + 0.5 * swiglu(rms_norm(x, ffn1_norm[l]), ffn1_w_gu[l], ffn1_w_down[l])
        h = rms_norm(x, mix_norm[l])
        cols_a, cols_b, cols_c, cols_g = split_cols(h @ w_in[l], (A_COLS, B_COLS, C_COLS, G_COLS))
        qa, ka, va = (to_heads(t, A_HEADS) for t in jnp.split(cols_a, 3, axis=-1))
        y_a = mixer_dilated_attention(qa, ka, va)
        qkv_b, z_b, alpha_b, beta_b = split_cols(cols_b, (3 * B_WIDTH, B_WIDTH, B_HEADS, B_HEADS))
        y_b = mixer_gated_deltanet(qkv_b, z_b, alpha_b, beta_b, b_conv[l], b_a_log[l], b_dt_bias[l], b_norm[l])
        y_c = mixer_rwkv7(cols_c, c_mu[l], c_w0[l], c_w2[l], c_a0[l], c_a2[l], c_g2[l],
                          c_k_k[l], c_k_a[l], c_r_k[l], c_gn_w[l], c_gn_b[l])
        g_a, g_b, g_c = jnp.split(jax.nn.sigmoid(cols_g), 3, axis=-1)
        merged = g_a * (y_a @ proj_a[l]) + g_b * (y_b @ proj_b[l]) + g_c * (y_c @ proj_c[l])
        x = x + merged @ w_out[l]
        x = x + 0.5 * swiglu(rms_norm(x, ffn2_norm[l]), ffn2_w_gu[l], ffn2_w_down[l])
    return rms_norm(x, final_norm)
```

```python
import functools

import jax
import jax.numpy as jnp
from jax import lax
from jax.experimental import pallas as pl
from jax.experimental.pallas import tpu as pltpu

F32 = jnp.float32
BF16 = jnp.bfloat16

D_MODEL = 2048
D_FF = 5632
A_DILATIONS = (1, 4, 16)
A_HPG = 4
A_HEADS = 12
HEAD_A = 128
A_OUT = A_HPG * HEAD_A
A_STEPS = 128
ALIBI_MAX_BIAS = 8.0
B_HEADS = 8
B_DIM = 128
B_WIDTH = B_HEADS * B_DIM
B_CONV = 4
C_DIM = 64
C_HEADS = 16
C_WIDTH = C_HEADS * C_DIM
C_DECAY_LORA = 96
C_ICLR_LORA = 96
C_GATE_LORA = 256
C_GN_EPS = 64e-5
NORM_EPS = 1e-6
L2_EPS = 1e-6
NEG_INF = -1e30
CHUNK = 64

P_BQ, P_BK, P_BV, P_BZ = 0, 1024, 2048, 3072
P_CR, P_CK, P_CV = 4096, 5120, 6144
P_AQ, P_AK, P_AV = 7168, 8704, 10240
P_G = 11776
P_SM = 17920
P_COLS = 18432
SM_W = 512
SM_ALPHA, SM_BETA, SM_XA, SM_XG = 96, 104, 128, 256

VMEM_LIMIT = 56 * 1024 * 1024


def _cparams(sem):
    return pltpu.CompilerParams(dimension_semantics=sem, vmem_limit_bytes=VMEM_LIMIT)


def _mm(a, b):
    return jnp.dot(a.astype(BF16), b.astype(BF16), preferred_element_type=F32)


def _mm_nt(a, b):
    return lax.dot_general(a.astype(BF16), b.astype(BF16), (((1,), (1,)), ((), ())),
                           preferred_element_type=F32)


def _mm_tn(a, b):
    return lax.dot_general(a.astype(BF16), b.astype(BF16), (((0,), (0,)), ((), ())),
                           preferred_element_type=F32)


def _silu(x):
    return x * jax.nn.sigmoid(x)


def _softplus(x):
    return jnp.maximum(x, 0.0) + jnp.log1p(jnp.exp(-jnp.abs(x)))


def _rms_rows(x, g):
    ms = jnp.mean(x * x, axis=-1, keepdims=True)
    return x * lax.rsqrt(ms + NORM_EPS) * g


def _ffn_body(x_ref, g_ref, wg_ref, wu_ref, wd_ref, o_ref, h_ref):
    @pl.when(pl.program_id(1) == 0)
    def _():
        x = x_ref[...]
        h_ref[...] = _rms_rows(x, g_ref[...]).astype(BF16)
        o_ref[...] = x

    h = h_ref[...]
    gate = jnp.dot(h, wg_ref[...], preferred_element_type=F32)
    up = jnp.dot(h, wu_ref[...], preferred_element_type=F32)
    act = (0.5 * _silu(gate) * up).astype(BF16)
    o_ref[...] += jnp.dot(act, wd_ref[...], preferred_element_type=F32)


def _ffn(x, g, w_gu, w_down, tm=512, tf=512):
    T, D = x.shape
    dff = w_down.shape[0]
    nj = dff // tf
    return pl.pallas_call(
        _ffn_body,
        grid=(T // tm, nj),
        in_specs=[
            pl.BlockSpec((tm, D), lambda i, j: (i, 0)),
            pl.BlockSpec((1, D), lambda i, j: (0, 0)),
            pl.BlockSpec((D, tf), lambda i, j: (0, j)),
            pl.BlockSpec((D, tf), lambda i, j: (0, j + nj)),
            pl.BlockSpec((tf, D), lambda i, j: (j, 0)),
        ],
        out_specs=pl.BlockSpec((tm, D), lambda i, j: (i, 0)),
        out_shape=jax.ShapeDtypeStruct((T, D), F32),
        scratch_shapes=[pltpu.VMEM((tm, D), BF16)],
        compiler_params=_cparams(("parallel", "arbitrary")),
        name="ffn",
    )(x, g.reshape(1, D), w_gu, w_gu, w_down)


def _inproj_body(x_ref, g_ref, w_ref, o_ref, h_ref):
    @pl.when(pl.program_id(1) == 0)
    def _():
        h_ref[...] = _rms_rows(x_ref[...], g_ref[...]).astype(BF16)

    o_ref[...] = jnp.dot(h_ref[...], w_ref[...], preferred_element_type=F32)


def _inproj(x, g, w, tm=1024, tn=512):
    T, D = x.shape
    N = w.shape[1]
    return pl.pallas_call(
        _inproj_body,
        grid=(T // tm, N // tn),
        in_specs=[
            pl.BlockSpec((tm, D), lambda i, j: (i, 0)),
            pl.BlockSpec((1, D), lambda i, j: (0, 0)),
            pl.BlockSpec((D, tn), lambda i, j: (0, j)),
        ],
        out_specs=pl.BlockSpec((tm, tn), lambda i, j: (i, j)),
        out_shape=jax.ShapeDtypeStruct((T, N), F32),
        scratch_shapes=[pltpu.VMEM((tm, D), BF16)],
        compiler_params=_cparams(("parallel", "arbitrary")),
        name="inproj",
    )(x, g.reshape(1, D), w)


def _attn_body(q_ref, kp_ref, kc_ref, vp_ref, vc_ref, o_ref, lse_ref, *, slopes, dil):
    blk = A_STEPS
    j = pl.program_id(1)
    qi = lax.broadcasted_iota(jnp.int32, (blk, 2 * blk), 0)
    ki = lax.broadcasted_iota(jnp.int32, (blk, 2 * blk), 1)
    delta = qi + blk - ki
    valid = (delta >= 0) & (delta <= A_STEPS) & ((ki >= blk) | (j > 0))
    dist = (delta * dil).astype(F32)
    for h in range(A_HPG):
        hs = slice(h * HEAD_A, (h + 1) * HEAD_A)
        q = q_ref[:, hs]
        k = jnp.concatenate([kp_ref[:, hs], kc_ref[:, hs]], axis=0)
        v = jnp.concatenate([vp_ref[:, hs], vc_ref[:, hs]], axis=0)
        s = _mm_nt(q, k) * (HEAD_A ** -0.5)
        s = jnp.where(valid, s - slopes[h] * dist, NEG_INF)
        m = jnp.max(s, axis=-1, keepdims=True)
        p = jnp.exp(s - m)
        l = jnp.sum(p, axis=-1, keepdims=True)
        o = _mm(p, v) / l
        o_ref[:, hs] = o
        lse_ref[:, hs] = jnp.broadcast_to(m + jnp.log(l), (blk, HEAD_A))


def _attn_group(q, k, v, gi):
    N, L, W = q.shape
    blk = A_STEPS
    slopes = tuple(2.0 ** (-ALIBI_MAX_BIAS * (gi * A_HPG + h + 1.0) / A_HEADS) for h in range(A_HPG))
    cur = pl.BlockSpec((None, blk, W), lambda n, j: (n, j, 0))
    prev = pl.BlockSpec((None, blk, W), lambda n, j: (n, jnp.maximum(j - 1, 0), 0))
    return pl.pallas_call(
        functools.partial(_attn_body, slopes=slopes, dil=A_DILATIONS[gi]),
        grid=(N, L // blk),
        in_specs=[cur, prev, cur, prev, cur],
        out_specs=[cur, cur],
        out_shape=[jax.ShapeDtypeStruct((N, L, W), F32)] * 2,
        compiler_params=_cparams(("parallel", "arbitrary")),
        name=f"attn_g{gi}",
    )(q, k, k, v, v)


def _gdn_body(alog_ref, dtb_ref, q_ref, k_ref, v_ref, z_ref, sm_ref, cq_ref, ck_ref, cv_ref, nw_ref,
              o_ref, S_ref, bq, bk, bv, *, tS):
    C = CHUNK
    h = pl.program_id(1)
    s = pl.program_id(2)

    @pl.when(s == 0)
    def _():
        S_ref[...] = jnp.zeros_like(S_ref)
        for buf in (bq, bk, bv):
            buf[0:8, :] = jnp.zeros((8, B_DIM), F32)

    def conv_silu(raw_ref, buf, cw_ref):
        buf[8:8 + tS, :] = raw_ref[...]
        y = cw_ref[0:1, :] * buf[5:5 + tS, :]
        for j in range(1, B_CONV):
            y = y + cw_ref[j:j + 1, :] * buf[5 + j:5 + j + tS, :]
        buf[0:8, :] = buf[tS:tS + 8, :]
        return _silu(y)

    q_all = conv_silu(q_ref, bq, cq_ref)
    k_all = conv_silu(k_ref, bk, ck_ref)
    v_all = conv_silu(v_ref, bv, cv_ref)
    q_all = q_all * lax.rsqrt(jnp.sum(q_all * q_all, axis=-1, keepdims=True) + L2_EPS) * (B_DIM ** -0.5)
    k_all = k_all * lax.rsqrt(jnp.sum(k_all * k_all, axis=-1, keepdims=True) + L2_EPS)

    sm = sm_ref[:, 0:128]
    lane = lax.broadcasted_iota(jnp.int32, sm.shape, 1)
    alpha = jnp.sum(jnp.where(lane == SM_ALPHA + h, sm, 0.0), axis=-1, keepdims=True)
    beta_logit = jnp.sum(jnp.where(lane == SM_BETA + h, sm, 0.0), axis=-1, keepdims=True)
    a_neg = -jnp.exp(jnp.full((1, 1), alog_ref[h], F32))
    ld_all = a_neg * _softplus(alpha + dtb_ref[h])
    beta_all = jax.nn.sigmoid(beta_logit)

    ri = lax.broadcasted_iota(jnp.int32, (C, C), 0)
    ci = lax.broadcasted_iota(jnp.int32, (C, C), 1)
    tril = ri >= ci
    strict = ri > ci
    eye = ri == ci
    eye_f = jnp.where(eye, 1.0, 0.0).astype(F32)
    nw = nw_ref[...]

    for c in range(tS // C):
        sl = slice(c * C, (c + 1) * C)
        q, k, v = q_all[sl], k_all[sl], v_all[sl]
        beta = beta_all[sl]
        LD = jnp.broadcast_to(ld_all[sl], (C, C))
        g_row = jnp.sum(jnp.where(ri <= ci, LD, 0.0), axis=0, keepdims=True)
        ld_row = jnp.sum(jnp.where(eye, LD, 0.0), axis=0, keepdims=True)
        g_col = jnp.sum(jnp.where(tril, jnp.broadcast_to(ld_row, (C, C)), 0.0), axis=1, keepdims=True)
        g_last = g_col[C - 1:C, :]
        decay = jnp.exp(jnp.where(tril, g_col - g_row, NEG_INF))
        eg = jnp.exp(g_col)
        kb = k * beta
        vb = v * beta
        m = jnp.where(strict, _mm_nt(kb, k) * decay, 0.0)
        pw = -m
        tinv = eye_f + pw
        for _ in range(5):
            pw = _mm(pw, pw)
            tinv = tinv + _mm(tinv, pw)
        sol = _mm(tinv, jnp.concatenate([vb, kb * eg], axis=1))
        u, w = sol[:, :B_DIM], sol[:, B_DIM:]
        att = jnp.where(tril, _mm_nt(q, k) * decay, 0.0)
        S = S_ref[...]
        v_new = u - _mm(w, S)
        o = _mm(q * eg, S) + _mm(att, v_new)
        S_ref[...] = S * jnp.exp(g_last) + _mm_tn(k * jnp.exp(g_last - g_col), v_new)
        o = _rms_rows(o, nw) * _silu(z_ref[sl, :])
        o_ref[sl, :] = o


def _gdn(cols, b_conv, a_log, dt_bias, norm_w, Bn, S, tS=256):
    T = cols.shape[0]
    nS = S // tS

    def col(off):
        return pl.BlockSpec((tS, B_DIM), lambda b, h, s: (b * nS + s, off // B_DIM + h))

    def cw(off):
        return pl.BlockSpec((B_CONV, B_DIM), lambda b, h, s: (0, off // B_DIM + h))

    smem = pl.BlockSpec(memory_space=pltpu.SMEM)
    return pl.pallas_call(
        functools.partial(_gdn_body, tS=tS),
        grid=(Bn, B_HEADS, nS),
        in_specs=[smem, smem, col(P_BQ), col(P_BK), col(P_BV), col(P_BZ),
                  pl.BlockSpec((tS, SM_W), lambda b, h, s: (b * nS + s, P_SM // SM_W)),
                  cw(0), cw(B_WIDTH), cw(2 * B_WIDTH),
                  pl.BlockSpec((1, B_DIM), lambda b, h, s: (0, 0))],
        out_specs=pl.BlockSpec((tS, B_DIM), lambda b, h, s: (b * nS + s, h)),
        out_shape=jax.ShapeDtypeStruct((T, B_WIDTH), F32),
        scratch_shapes=[pltpu.VMEM((B_DIM, B_DIM), F32)] + [pltpu.VMEM((tS + 8, B_DIM), F32)] * 3,
        compiler_params=_cparams(("parallel", "parallel", "arbitrary")),
        name="gdn",
    )(a_log, dt_bias, cols, cols, cols, cols, cols, b_conv, b_conv, b_conv, norm_w.reshape(1, B_DIM))


def _rwkv_prep_body(r_ref, k_ref, v_ref, sm_ref, mur_ref, muk_ref, muv_ref, musm_ref,
                    w0_ref, w2_ref, a0_ref, a2_ref, g2_ref, kkw_ref, kaw_ref,
                    r_o, k_o, v_o, lw_o, kk_o, a_o, g_o, br, bk, bv, bsm, *, tS):
    s = pl.program_id(1)

    @pl.when(s == 0)
    def _():
        for buf in (br, bk, bv, bsm):
            buf[0:8, :] = jnp.zeros((8, buf.shape[1]), F32)

    def shift(x_ref, buf, mu_ref):
        x = x_ref[...]
        buf[8:8 + tS, :] = x
        prev = buf[7:7 + tS, :]
        buf[0:8, :] = buf[tS:tS + 8, :]
        return x + (prev - x) * mu_ref[...]

    r = shift(r_ref, br, mur_ref)
    k = shift(k_ref, bk, muk_ref)
    v = shift(v_ref, bv, muv_ref)
    sm = shift(sm_ref, bsm, musm_ref)
    xw, xa, xg = sm[:, 0:128], sm[:, SM_XA:SM_XA + 128], sm[:, SM_XG:SM_XG + C_GATE_LORA]
    w = -_softplus(-(w0_ref[...] + _mm(jnp.tanh(xw), w2_ref[...]))) - 0.5
    a = jax.nn.sigmoid(a0_ref[...] + _mm(xa, a2_ref[...]))
    r_o[...] = r
    v_o[...] = v
    lw_o[...] = -jnp.exp(w)
    a_o[...] = a
    g_o[...] = _mm(jax.nn.sigmoid(xg), g2_ref[...])
    kk_o[...] = k * kkw_ref[...]
    k_o[...] = k * (1.0 + (a - 1.0) * kaw_ref[...])


def _rwkv_prep(cols, mu_r, mu_k, mu_v, mu_sm, w0, w2p, a0, a2p, g2, k_k, k_a, Bn, S, tS=256):
    T = cols.shape[0]
    nS = S // tS
    W = C_WIDTH

    def col(off, width):
        return pl.BlockSpec((tS, width), lambda b, s: (b * nS + s, off // width))

    def full(shape):
        return pl.BlockSpec(shape, lambda b, s: (0, 0))

    out = pl.BlockSpec((tS, W), lambda b, s: (b * nS + s, 0))
    return pl.pallas_call(
        functools.partial(_rwkv_prep_body, tS=tS),
        grid=(Bn, nS),
        in_specs=[col(P_CR, W), col(P_CK, W), col(P_CV, W), col(P_SM, SM_W),
                  full((1, W)), full((1, W)), full((1, W)), full((1, SM_W)),
                  full((1, W)), full((128, W)), full((1, W)), full((128, W)), full((C_GATE_LORA, W)),
                  full((1, W)), full((1, W))],
        out_specs=[out] * 7,
        out_shape=[jax.ShapeDtypeStruct((T, W), F32)] * 7,
        scratch_shapes=[pltpu.VMEM((tS + 8, W), F32)] * 3 + [pltpu.VMEM((tS + 8, SM_W), F32)],
        compiler_params=_cparams(("parallel", "arbitrary")),
        name="rwkv_prep",
    )(cols, cols, cols, cols, mu_r, mu_k, mu_v, mu_sm, w0, w2p, a0, a2p, g2, k_k, k_a)


def _rwkv_rec_body(r_ref, k_ref, v_ref, lw_ref, kk_ref, a_ref, g_ref, rk_ref, gnw_ref, gnb_ref,
                   o_ref, S_ref, *, tS):
    C = CHUNK
    P = 2 * C_DIM
    s = pl.program_id(2)

    @pl.when(s == 0)
    def _():
        S_ref[...] = jnp.zeros_like(S_ref)

    m0 = lax.broadcasted_iota(jnp.int32, (1, P), 1) < C_DIM
    ri = lax.broadcasted_iota(jnp.int32, (C, C), 0)
    ci = lax.broadcasted_iota(jnp.int32, (C, C), 1)
    tril_b = jnp.where(ri >= ci, 1.0, 0.0).astype(BF16)
    r2 = lax.broadcasted_iota(jnp.int32, (2 * C, 2 * C), 0)
    c2 = lax.broadcasted_iota(jnp.int32, (2 * C, 2 * C), 1)
    strict2 = r2 > c2
    eye2 = jnp.where(r2 == c2, 1.0, 0.0).astype(F32)
    rt = lax.broadcasted_iota(jnp.int32, (C, 2 * C), 0)
    ct = lax.broadcasted_iota(jnp.int32, (C, 2 * C), 1)
    incl = rt >= (ct & (C - 1))

    def seg_sum(x):
        s0 = jnp.sum(jnp.where(m0, x, 0.0), axis=-1, keepdims=True)
        s1 = jnp.sum(jnp.where(m0, 0.0, x), axis=-1, keepdims=True)
        return jnp.where(m0, s0, s1)

    def stack(x):
        return jnp.concatenate([jnp.where(m0, x, 0.0), jnp.where(m0, 0.0, x)], axis=0)

    rk = rk_ref[...]
    gnw = gnw_ref[...]
    gnb = gnb_ref[...]

    for c in range(tS // C):
        sl = slice(c * C, (c + 1) * C)
        r, k, v, lw, a = r_ref[sl, :], k_ref[sl, :], v_ref[sl, :], lw_ref[sl, :], a_ref[sl, :]
        kkraw = kk_ref[sl, :]
        kk = kkraw * lax.rsqrt(seg_sum(kkraw * kkraw) + L2_EPS)
        bv = kk * a
        lw_hi = lw.astype(BF16)
        lw_lo = (lw - lw_hi.astype(F32)).astype(BF16)
        cum = (jnp.dot(tril_b, lw_hi, preferred_element_type=F32)
               + jnp.dot(tril_b, lw_lo, preferred_element_type=F32))
        w_inc = jnp.exp(cum)
        w_inv = jnp.exp(-cum)
        w_end = w_inc[C - 1:C, :]
        r_t = r * w_inc
        k_t = k * w_inv
        b_t = bv * w_inv
        a_t = -kk * jnp.exp(cum - lw)
        a_st = stack(a_t).astype(BF16)
        b_st = stack(b_t)
        k_st = stack(k_t)
        v_st = stack(v).astype(BF16)
        bk_st = jnp.concatenate([b_st, k_st], axis=0).astype(BF16)
        aa = _mm_nt(a_st, bk_st)
        a_ab = jnp.where(strict2, aa[:, :2 * C], 0.0)
        a_ak = jnp.where(strict2, aa[:, 2 * C:], 0.0)
        rr = _mm_nt(r_t, bk_st)
        a_rb = jnp.where(incl, rr[:, :2 * C], 0.0)
        a_rk = jnp.where(incl, rr[:, 2 * C:], 0.0)
        pw = a_ab
        tinv = eye2 + pw
        for _ in range(5):
            pw = _mm(pw, pw)
            tinv = tinv + _mm(tinv, pw)
        S = S_ref[...]
        S_b = S.astype(BF16)
        u = _mm(tinv, _mm_nt(a_st, S_b) + _mm(a_ak, v_st))
        o = _mm_nt(r_t, S_b) + _mm(a_rb, u) + _mm(a_rk, v_st)
        S_ref[...] = S * w_end + _mm_tn(u, b_st * w_end) + _mm_tn(v_st, k_st * w_end)
        mean = seg_sum(o) * (1.0 / C_DIM)
        d = o - mean
        var = seg_sum(d * d) * (1.0 / C_DIM)
        on = d * lax.rsqrt(var + C_GN_EPS) * gnw + gnb
        bonus = seg_sum(r * k * rk) * v
        o_ref[sl, :] = (on + bonus) * g_ref[sl, :]


def _rwkv_rec(r, k, v, lw, kk, a, g, r_k, gn_w, gn_b, Bn, S, tS=256):
    T = r.shape[0]
    nS = S // tS
    P = 2 * C_DIM
    blk = pl.BlockSpec((tS, P), lambda b, h, s: (b * nS + s, h))
    par = pl.BlockSpec((1, P), lambda b, h, s: (0, h))
    return pl.pallas_call(
        functools.partial(_rwkv_rec_body, tS=tS),
        grid=(Bn, C_WIDTH // P, nS),
        in_specs=[blk] * 7 + [par] * 3,
        out_specs=blk,
        out_shape=jax.ShapeDtypeStruct((T, C_WIDTH), F32),
        scratch_shapes=[pltpu.VMEM((P, P), F32)],
        compiler_params=_cparams(("parallel", "parallel", "arbitrary")),
        name="rwkv_rec",
    )(r, k, v, lw, kk, a, g, r_k, gn_w, gn_b)


def _merge_body(x_ref, o0_ref, o1_ref, o2_ref, l0_ref, l1_ref, l2_ref, yb_ref, yc_ref,
                ga_ref, gb_ref, gc_ref, pa_ref, pb_ref, pc_ref, wo_ref, o_ref, ya_s, yb_s, yc_s):
    @pl.when(pl.program_id(1) == 0)
    def _():
        l0, l1, l2 = l0_ref[...], l1_ref[...], l2_ref[...]
        m = jnp.maximum(jnp.maximum(l0, l1), l2)
        e0, e1, e2 = jnp.exp(l0 - m), jnp.exp(l1 - m), jnp.exp(l2 - m)
        ya = (e0 * o0_ref[...] + e1 * o1_ref[...] + e2 * o2_ref[...]) / (e0 + e1 + e2)
        ya_s[...] = ya.astype(BF16)
        yb_s[...] = yb_ref[...].astype(BF16)
        yc_s[...] = yc_ref[...].astype(BF16)
        o_ref[...] = x_ref[...]

    merged = (jax.nn.sigmoid(ga_ref[...]) * jnp.dot(ya_s[...], pa_ref[...], preferred_element_type=F32)
              + jax.nn.sigmoid(gb_ref[...]) * jnp.dot(yb_s[...], pb_ref[...], preferred_element_type=F32)
              + jax.nn.sigmoid(gc_ref[...]) * jnp.dot(yc_s[...], pc_ref[...], preferred_element_type=F32))
    o_ref[...] += jnp.dot(merged.astype(BF16), wo_ref[...], preferred_element_type=F32)


def _merge(x, o_g, lse_g, y_b, y_c, cols, proj_a, proj_b, proj_c, w_out, tm=512, tn=512):
    T, D = x.shape
    nj = D // tn

    def rows(width):
        return pl.BlockSpec((tm, width), lambda i, j: (i, 0))

    def gate(k):
        return pl.BlockSpec((tm, tn), lambda i, j: (i, P_G // tn + k * nj + j))

    def wcol(kdim):
        return pl.BlockSpec((kdim, tn), lambda i, j: (0, j))

    return pl.pallas_call(
        _merge_body,
        grid=(T // tm, nj),
        in_specs=[rows(D)] + [rows(A_OUT)] * 6 + [rows(B_WIDTH), rows(C_WIDTH),
                  gate(0), gate(1), gate(2), wcol(A_OUT), wcol(B_WIDTH), wcol(C_WIDTH),
                  pl.BlockSpec((tn, D), lambda i, j: (j, 0))],
        out_specs=rows(D),
        out_shape=jax.ShapeDtypeStruct((T, D), F32),
        scratch_shapes=[pltpu.VMEM((tm, A_OUT), BF16), pltpu.VMEM((tm, B_WIDTH), BF16),
                        pltpu.VMEM((tm, C_WIDTH), BF16)],
        compiler_params=_cparams(("parallel", "arbitrary")),
        name="merge",
    )(x, *o_g, *lse_g, y_b, y_c, cols, cols, cols, proj_a, proj_b, proj_c, w_out)


def _norm_body(x_ref, g_ref, o_ref):
    o_ref[...] = _rms_rows(x_ref[...], g_ref[...])


def _final_norm(x, g, tm=1024):
    T, D = x.shape
    return pl.pallas_call(
        _norm_body,
        grid=(T // tm,),
        in_specs=[pl.BlockSpec((tm, D), lambda i: (i, 0)), pl.BlockSpec((1, D), lambda i: (0, 0))],
        out_specs=pl.BlockSpec((tm, D), lambda i: (i, 0)),
        out_shape=jax.ShapeDtypeStruct((T, D), F32),
        compiler_params=_cparams(("parallel",)),
        name="final_norm",
    )(x, g.reshape(1, D))


def _pack_w_in(w):
    a_cols = 3 * A_HEADS * HEAD_A
    b0 = a_cols
    b_ab = b0 + 4 * B_WIDTH
    c0 = b_ab + 2 * B_HEADS
    c_xw = c0 + 3 * C_WIDTH
    c_xa = c_xw + C_DECAY_LORA
    c_xg = c_xa + C_ICLR_LORA
    g0 = c_xg + C_GATE_LORA
    z = lambda n: jnp.zeros((w.shape[0], n), w.dtype)
    return jnp.concatenate([
        w[:, b0:b_ab], w[:, c0:c_xw], w[:, 0:a_cols], w[:, g0:],
        w[:, c_xw:c_xa], w[:, b_ab:c0], z(16), w[:, c_xa:c_xg], z(32), w[:, c_xg:g0],
    ], axis=1).astype(BF16)


def _pad_rows(w, n):
    return jnp.concatenate([w, jnp.zeros((n - w.shape[0], w.shape[1]), w.dtype)], axis=0)


def _mixers(x, cols, l, Bn, S, b_conv, b_a_log, b_dt_bias, b_norm, c_mu, c_w0, c_w2, c_a0, c_a2, c_g2,
            c_k_k, c_k_a, c_r_k, c_gn_w, c_gn_b):
    T = Bn * S
    o_g, lse_g = [], []
    for gi, d in enumerate(A_DILATIONS):
        L = S // d

        def to_res(off):
            t = cols[:, off + gi * A_OUT: off + (gi + 1) * A_OUT].reshape(Bn, L, d, A_OUT)
            return t.transpose(0, 2, 1, 3).reshape(Bn * d, L, A_OUT).astype(BF16)

        def from_res(t):
            return t.reshape(Bn, d, L, A_OUT).transpose(0, 2, 1, 3).reshape(T, A_OUT)

        o, lse = _attn_group(to_res(P_AQ), to_res(P_AK), to_res(P_AV), gi)
        o_g.append(from_res(o))
        lse_g.append(from_res(lse))
    y_b = _gdn(cols, b_conv[l], b_a_log[l], b_dt_bias[l], b_norm[l], Bn, S)
    W = C_WIDTH
    mu = c_mu[l]
    z = lambda n: jnp.zeros((n,), F32)
    mu_sm = jnp.concatenate([mu[3 * W:3 * W + C_DECAY_LORA], z(32),
                             mu[3 * W + C_DECAY_LORA:3 * W + C_DECAY_LORA + C_ICLR_LORA], z(32),
                             mu[3 * W + C_DECAY_LORA + C_ICLR_LORA:]])
    row = lambda t: t.reshape(1, -1)
    prep = _rwkv_prep(cols, row(mu[0:W]), row(mu[W:2 * W]), row(mu[2 * W:3 * W]), row(mu_sm),
                      row(c_w0[l]), _pad_rows(c_w2[l], 128).astype(BF16), row(c_a0[l]),
                      _pad_rows(c_a2[l], 128).astype(BF16), c_g2[l].astype(BF16),
                      row(c_k_k[l]), row(c_k_a[l]), Bn, S)
    y_c = _rwkv_rec(*prep, row(c_r_k[l]), row(c_gn_w[l]), row(c_gn_b[l]), Bn, S)
    return o_g, lse_g, y_b, y_c


def kernel(x, ffn1_norm, ffn1_w_gu, ffn1_w_down, mix_norm, w_in, b_conv, b_a_log, b_dt_bias, b_norm,
           c_mu, c_w0, c_w2, c_a0, c_a2, c_g2, c_k_k, c_k_a, c_r_k, c_gn_w, c_gn_b,
           proj_a, proj_b, proj_c, w_out, ffn2_norm, ffn2_w_gu, ffn2_w_down, final_norm):
    Bn, S, D = x.shape
    depth = w_in.shape[0]
    x = x.reshape(Bn * S, D)
    bf = lambda t: t.astype(BF16)
    for l in range(depth):
        x = _ffn(x, ffn1_norm[l], bf(ffn1_w_gu[l]), bf(ffn1_w_down[l]))
        cols = _inproj(x, mix_norm[l], _pack_w_in(w_in[l]))
        o_g, lse_g, y_b, y_c = _mixers(x, cols, l, Bn, S, b_conv, b_a_log, b_dt_bias, b_norm, c_mu, c_w0,
                                       c_w2, c_a0, c_a2, c_g2, c_k_k, c_k_a, c_r_k, c_gn_w, c_gn_b)
        x = _merge(x, o_g, lse_g, y_b, y_c, cols, bf(proj_a[l]), bf(proj_b[l]), bf(proj_c[l]), bf(w_out[l]))
        x = _ffn(x, ffn2_norm[l], bf(ffn2_w_gu[l]), bf(ffn2_w_down[l]))
    return _final_norm(x, final_norm).reshape(Bn, S, D)
```

```python
import functools

import jax
import jax.numpy as jnp
from jax import lax
from jax.experimental import pallas as pl
from jax.experimental.pallas import tpu as pltpu

F32 = jnp.float32
BF16 = jnp.bfloat16

D_MODEL = 2048
D_FF = 5632
A_DILATIONS = (1, 4, 16)
A_HPG = 4
A_HEADS = 12
HEAD_A = 128
A_OUT = A_HPG * HEAD_A
A_STEPS = 128
ALIBI_MAX_BIAS = 8.0
B_HEADS = 8
B_DIM = 128
B_WIDTH = B_HEADS * B_DIM
B_CONV = 4
C_DIM = 64
C_HEADS = 16
C_WIDTH = C_HEADS * C_DIM
C_DECAY_LORA = 96
C_ICLR_LORA = 96
C_GATE_LORA = 256
C_GN_EPS = 64e-5
NORM_EPS = 1e-6
L2_EPS = 1e-6
NEG_INF = -1e30
CHUNK = 64

P_BQ, P_BK, P_BV, P_BZ = 0, 1024, 2048, 3072
P_CR, P_CK, P_CV = 4096, 5120, 6144
P_AQ, P_AK, P_AV = 7168, 8704, 10240
P_G = 11776
P_SM = 17920
P_COLS = 18432
SM_W = 512
SM_ALPHA, SM_BETA, SM_XA, SM_XG = 96, 104, 128, 256

VMEM_LIMIT = 56 * 1024 * 1024


def _cparams(sem):
    return pltpu.CompilerParams(dimension_semantics=sem, vmem_limit_bytes=VMEM_LIMIT)


def _mm(a, b):
    return jnp.dot(a.astype(BF16), b.astype(BF16), preferred_element_type=F32)


def _mm_nt(a, b):
    return lax.dot_general(a.astype(BF16), b.astype(BF16), (((1,), (1,)), ((), ())),
                           preferred_element_type=F32)


def _mm_tn(a, b):
    return lax.dot_general(a.astype(BF16), b.astype(BF16), (((0,), (0,)), ((), ())),
                           preferred_element_type=F32)


def _bmm(a, b):
    return jnp.einsum("bmk,bkn->bmn", a.astype(BF16), b.astype(BF16), preferred_element_type=F32)


def _bmm_nt(a, b):
    return jnp.einsum("bmk,bnk->bmn", a.astype(BF16), b.astype(BF16), preferred_element_type=F32)


def _bmm_tn(a, b):
    return jnp.einsum("bkm,bkn->bmn", a.astype(BF16), b.astype(BF16), preferred_element_type=F32)


def _silu(x):
    return x * jax.nn.sigmoid(x)


def _softplus(x):
    return jnp.maximum(x, 0.0) + jnp.log1p(jnp.exp(-jnp.abs(x)))


def _rms_rows(x, g):
    ms = jnp.mean(x * x, axis=-1, keepdims=True)
    return x * lax.rsqrt(ms + NORM_EPS) * g


def _ffn_body(x_ref, g_ref, wg_ref, wu_ref, wd_ref, o_ref, h_ref):
    @pl.when(pl.program_id(1) == 0)
    def _():
        x = x_ref[...]
        h_ref[...] = _rms_rows(x, g_ref[...]).astype(BF16)
        o_ref[...] = x

    h = h_ref[...]
    gate = jnp.dot(h, wg_ref[...], preferred_element_type=F32)
    up = jnp.dot(h, wu_ref[...], preferred_element_type=F32)
    act = (0.5 * _silu(gate) * up).astype(BF16)
    o_ref[...] += jnp.dot(act, wd_ref[...], preferred_element_type=F32)


def _ffn(x, g, w_gu, w_down, l, tm=512, tf=512):
    T, D = x.shape
    dff = w_down.shape[1]
    nj = dff // tf
    return pl.pallas_call(
        _ffn_body,
        grid=(T // tm, nj),
        in_specs=[
            pl.BlockSpec((tm, D), lambda i, j: (i, 0)),
            pl.BlockSpec((1, D), lambda i, j: (0, 0)),
            pl.BlockSpec((None, D, tf), lambda i, j: (l, 0, j)),
            pl.BlockSpec((None, D, tf), lambda i, j: (l, 0, j + nj)),
            pl.BlockSpec((None, tf, D), lambda i, j: (l, j, 0)),
        ],
        out_specs=pl.BlockSpec((tm, D), lambda i, j: (i, 0)),
        out_shape=jax.ShapeDtypeStruct((T, D), F32),
        scratch_shapes=[pltpu.VMEM((tm, D), BF16)],
        compiler_params=_cparams(("parallel", "arbitrary")),
        name="ffn",
    )(x, g.reshape(1, D), w_gu, w_gu, w_down)


def _inproj_body(x_ref, g_ref, w_ref, o_ref, h_ref):
    @pl.when(pl.program_id(1) == 0)
    def _():
        h_ref[...] = _rms_rows(x_ref[...], g_ref[...]).astype(BF16)

    o_ref[...] = jnp.dot(h_ref[...], w_ref[...], preferred_element_type=F32)


def _inproj(x, g, w, tm=1024, tn=1024):
    T, D = x.shape
    N = w.shape[1]
    return pl.pallas_call(
        _inproj_body,
        grid=(T // tm, N // tn),
        in_specs=[
            pl.BlockSpec((tm, D), lambda i, j: (i, 0)),
            pl.BlockSpec((1, D), lambda i, j: (0, 0)),
            pl.BlockSpec((D, tn), lambda i, j: (0, j)),
        ],
        out_specs=pl.BlockSpec((tm, tn), lambda i, j: (i, j)),
        out_shape=jax.ShapeDtypeStruct((T, N), F32),
        scratch_shapes=[pltpu.VMEM((tm, D), BF16)],
        compiler_params=_cparams(("parallel", "arbitrary")),
        name="inproj",
    )(x, g.reshape(1, D), w)


def _attn_body(q_ref, kp_ref, kc_ref, vp_ref, vc_ref, o_ref, lse_ref, *, slopes, dil):
    blk = A_STEPS
    j = pl.program_id(1)
    qi = lax.broadcasted_iota(jnp.int32, (blk, 2 * blk), 0)
    ki = lax.broadcasted_iota(jnp.int32, (blk, 2 * blk), 1)
    delta = qi + blk - ki
    valid = (delta >= 0) & (delta <= A_STEPS) & ((ki >= blk) | (j > 0))
    dist = (delta * dil).astype(F32)
    for h in range(A_HPG):
        hs = slice(h * HEAD_A, (h + 1) * HEAD_A)
        q = q_ref[:, hs]
        k = jnp.concatenate([kp_ref[:, hs], kc_ref[:, hs]], axis=0)
        v = jnp.concatenate([vp_ref[:, hs], vc_ref[:, hs]], axis=0)
        s = _mm_nt(q, k) * (HEAD_A ** -0.5)
        s = jnp.where(valid, s - slopes[h] * dist, NEG_INF)
        m = jnp.max(s, axis=-1, keepdims=True)
        p = jnp.exp(s - m)
        l = jnp.sum(p, axis=-1, keepdims=True)
        o = _mm(p, v) / l
        o_ref[:, hs] = o
        lse_ref[:, hs] = jnp.broadcast_to(m + jnp.log(l), (blk, HEAD_A))


def _attn_group(q, k, v, gi):
    N, L, W = q.shape
    blk = A_STEPS
    slopes = tuple(2.0 ** (-ALIBI_MAX_BIAS * (gi * A_HPG + h + 1.0) / A_HEADS) for h in range(A_HPG))
    cur = pl.BlockSpec((None, blk, W), lambda n, j: (n, j, 0))
    prev = pl.BlockSpec((None, blk, W), lambda n, j: (n, jnp.maximum(j - 1, 0), 0))
    return pl.pallas_call(
        functools.partial(_attn_body, slopes=slopes, dil=A_DILATIONS[gi]),
        grid=(N, L // blk),
        in_specs=[cur, prev, cur, prev, cur],
        out_specs=[cur, cur],
        out_shape=[jax.ShapeDtypeStruct((N, L, W), F32)] * 2,
        compiler_params=_cparams(("parallel", "arbitrary")),
        name=f"attn_g{gi}",
    )(q, k, k, v, v)


def _gdn_body(alog_ref, dtb_ref, q_ref, k_ref, v_ref, z_ref, sm_ref, cq_ref, ck_ref, cv_ref, nw_ref,
              o_ref, S_ref, bq, bk, bv, *, tS):
    C = CHUNK
    h = pl.program_id(1)
    s = pl.program_id(2)

    @pl.when(s == 0)
    def _():
        S_ref[...] = jnp.zeros_like(S_ref)
        for buf in (bq, bk, bv):
            buf[0:8, :] = jnp.zeros((8, B_DIM), F32)

    def conv_silu(raw_ref, buf, cw_ref):
        buf[8:8 + tS, :] = raw_ref[...]
        y = cw_ref[0:1, :] * buf[5:5 + tS, :]
        for j in range(1, B_CONV):
            y = y + cw_ref[j:j + 1, :] * buf[5 + j:5 + j + tS, :]
        buf[0:8, :] = buf[tS:tS + 8, :]
        return _silu(y)

    q_all = conv_silu(q_ref, bq, cq_ref)
    k_all = conv_silu(k_ref, bk, ck_ref)
    v_all = conv_silu(v_ref, bv, cv_ref)
    q_all = q_all * lax.rsqrt(jnp.sum(q_all * q_all, axis=-1, keepdims=True) + L2_EPS) * (B_DIM ** -0.5)
    k_all = k_all * lax.rsqrt(jnp.sum(k_all * k_all, axis=-1, keepdims=True) + L2_EPS)

    sm = sm_ref[:, 0:128]
    lane = lax.broadcasted_iota(jnp.int32, sm.shape, 1)
    alpha = jnp.sum(jnp.where(lane == SM_ALPHA + h, sm, 0.0), axis=-1, keepdims=True)
    beta_logit = jnp.sum(jnp.where(lane == SM_BETA + h, sm, 0.0), axis=-1, keepdims=True)
    a_neg = -jnp.exp(jnp.full((1, 1), alog_ref[h], F32))
    ld_all = a_neg * _softplus(alpha + dtb_ref[h])
    beta_all = jax.nn.sigmoid(beta_logit)

    n = tS // C
    ri = lax.broadcasted_iota(jnp.int32, (n, C, C), 1)
    ci = lax.broadcasted_iota(jnp.int32, (n, C, C), 2)
    tril = ri >= ci
    strict = ri > ci
    eye = ri == ci
    eye_f = jnp.where(eye, 1.0, 0.0).astype(F32)

    q = q_all.reshape(n, C, B_DIM)
    k = k_all.reshape(n, C, B_DIM)
    v = v_all.reshape(n, C, B_DIM)
    beta = beta_all.reshape(n, C, 1)
    LD = jnp.broadcast_to(ld_all.reshape(n, C, 1), (n, C, C))
    g_row = jnp.sum(jnp.where(ri <= ci, LD, 0.0), axis=1, keepdims=True)
    ld_row = jnp.sum(jnp.where(eye, LD, 0.0), axis=1, keepdims=True)
    g_col = jnp.sum(jnp.where(tril, jnp.broadcast_to(ld_row, (n, C, C)), 0.0), axis=2, keepdims=True)
    g_last = g_col[:, C - 1:C, :]
    decay = jnp.exp(jnp.where(tril, g_col - g_row, NEG_INF))
    eg = jnp.exp(g_col)
    kb = k * beta
    vb = v * beta
    kq = _bmm_nt(jnp.concatenate([kb, q], axis=1), k)
    m = jnp.where(strict, kq[:, :C] * decay, 0.0)
    att = jnp.where(tril, kq[:, C:] * decay, 0.0)
    pw = -m
    tinv = eye_f + pw
    for _ in range(5):
        pw = _bmm(pw, pw)
        tinv = tinv + _bmm(tinv, pw)
    wu = _bmm(tinv, jnp.concatenate([kb * eg, vb], axis=2))
    k_dec = k * jnp.exp(g_last - g_col)
    gz = _bmm_tn(k_dec, wu)
    aw = _bmm(att, wu)
    q_eff = q * eg - aw[:, :, :B_DIM]
    o0 = aw[:, :, B_DIM:]
    cd = jnp.exp(g_last)
    S = S_ref[...]
    outs = []
    for c in range(n):
        outs.append(_mm(q_eff[c], S) + o0[c])
        S = S * cd[c] - _mm(gz[c, :, :B_DIM], S) + gz[c, :, B_DIM:]
    S_ref[...] = S
    o = jnp.concatenate(outs, axis=0)
    o_ref[...] = _rms_rows(o, nw_ref[...]) * _silu(z_ref[...])


def _gdn(cols, b_conv, a_log, dt_bias, norm_w, Bn, S, tS=1024):
    T = cols.shape[0]
    nS = S // tS

    def col(off):
        return pl.BlockSpec((tS, B_DIM), lambda b, h, s: (b * nS + s, off // B_DIM + h))

    def cw(off):
        return pl.BlockSpec((B_CONV, B_DIM), lambda b, h, s: (0, off // B_DIM + h))

    smem = pl.BlockSpec(memory_space=pltpu.SMEM)
    return pl.pallas_call(
        functools.partial(_gdn_body, tS=tS),
        grid=(Bn, B_HEADS, nS),
        in_specs=[smem, smem, col(P_BQ), col(P_BK), col(P_BV), col(P_BZ),
                  pl.BlockSpec((tS, SM_W), lambda b, h, s: (b * nS + s, P_SM // SM_W)),
                  cw(0), cw(B_WIDTH), cw(2 * B_WIDTH),
                  pl.BlockSpec((1, B_DIM), lambda b, h, s: (0, 0))],
        out_specs=pl.BlockSpec((tS, B_DIM), lambda b, h, s: (b * nS + s, h)),
        out_shape=jax.ShapeDtypeStruct((T, B_WIDTH), F32),
        scratch_shapes=[pltpu.VMEM((B_DIM, B_DIM), F32)] + [pltpu.VMEM((tS + 8, B_DIM), F32)] * 3,
        compiler_params=_cparams(("parallel", "parallel", "arbitrary")),
        name="gdn",
    )(a_log, dt_bias, cols, cols, cols, cols, cols, b_conv, b_conv, b_conv, norm_w.reshape(1, B_DIM))


def _rwkv_prep_body(r_ref, k_ref, v_ref, sm_ref, mur_ref, muk_ref, muv_ref, musm_ref,
                    w0_ref, w2_ref, a0_ref, a2_ref, g2_ref, kkw_ref, kaw_ref,
                    r_o, k_o, v_o, lw_o, kk_o, a_o, g_o, br, bk, bv, bsm, *, tS):
    s = pl.program_id(1)

    @pl.when(s == 0)
    def _():
        for buf in (br, bk, bv, bsm):
            buf[0:8, :] = jnp.zeros((8, buf.shape[1]), F32)

    def shift(x_ref, buf, mu_ref):
        x = x_ref[...]
        buf[8:8 + tS, :] = x
        prev = buf[7:7 + tS, :]
        buf[0:8, :] = buf[tS:tS + 8, :]
        return x + (prev - x) * mu_ref[...]

    r = shift(r_ref, br, mur_ref)
    k = shift(k_ref, bk, muk_ref)
    v = shift(v_ref, bv, muv_ref)
    sm = shift(sm_ref, bsm, musm_ref)
    xw, xa, xg = sm[:, 0:128], sm[:, SM_XA:SM_XA + 128], sm[:, SM_XG:SM_XG + C_GATE_LORA]
    w = -_softplus(-(w0_ref[...] + _mm(jnp.tanh(xw), w2_ref[...]))) - 0.5
    a = jax.nn.sigmoid(a0_ref[...] + _mm(xa, a2_ref[...]))
    r_o[...] = r
    v_o[...] = v
    lw_o[...] = -jnp.exp(w)
    a_o[...] = a
    g_o[...] = _mm(jax.nn.sigmoid(xg), g2_ref[...])
    kk_o[...] = k * kkw_ref[...]
    k_o[...] = k * (1.0 + (a - 1.0) * kaw_ref[...])


def _rwkv_prep(cols, mu_r, mu_k, mu_v, mu_sm, w0, w2p, a0, a2p, g2, k_k, k_a, Bn, S, tS=256):
    T = cols.shape[0]
    nS = S // tS
    W = C_WIDTH

    def col(off, width):
        return pl.BlockSpec((tS, width), lambda b, s: (b * nS + s, off // width))

    def full(shape):
        return pl.BlockSpec(shape, lambda b, s: (0, 0))

    out = pl.BlockSpec((tS, W), lambda b, s: (b * nS + s, 0))
    return pl.pallas_call(
        functools.partial(_rwkv_prep_body, tS=tS),
        grid=(Bn, nS),
        in_specs=[col(P_CR, W), col(P_CK, W), col(P_CV, W), col(P_SM, SM_W),
                  full((1, W)), full((1, W)), full((1, W)), full((1, SM_W)),
                  full((1, W)), full((128, W)), full((1, W)), full((128, W)), full((C_GATE_LORA, W)),
                  full((1, W)), full((1, W))],
        out_specs=[out] * 7,
        out_shape=[jax.ShapeDtypeStruct((T, W), F32)] * 7,
        scratch_shapes=[pltpu.VMEM((tS + 8, W), F32)] * 3 + [pltpu.VMEM((tS + 8, SM_W), F32)],
        compiler_params=_cparams(("parallel", "arbitrary")),
        name="rwkv_prep",
    )(cols, cols, cols, cols, mu_r, mu_k, mu_v, mu_sm, w0, w2p, a0, a2p, g2, k_k, k_a)


def _rwkv_rec_body(r_ref, k_ref, v_ref, lw_ref, kk_ref, a_ref, g_ref, rk_ref, gnw_ref, gnb_ref,
                   o_ref, S_ref, *, tS):
    C = CHUNK
    P = 2 * C_DIM
    s = pl.program_id(2)

    @pl.when(s == 0)
    def _():
        S_ref[...] = jnp.zeros_like(S_ref)

    n = tS // C
    m0 = lax.broadcasted_iota(jnp.int32, (1, P), 1) < C_DIM
    rs = lax.broadcasted_iota(jnp.int32, (tS, tS), 0)
    cs = lax.broadcasted_iota(jnp.int32, (tS, tS), 1)
    tril_b = jnp.where((rs >= cs) & ((rs ^ cs) < C), 1.0, 0.0).astype(BF16)
    r2 = lax.broadcasted_iota(jnp.int32, (n, 2 * C, 2 * C), 1)
    c2 = lax.broadcasted_iota(jnp.int32, (n, 2 * C, 2 * C), 2)
    strict2 = r2 > c2
    eye2 = jnp.where(r2 == c2, 1.0, 0.0).astype(F32)
    rt = lax.broadcasted_iota(jnp.int32, (n, C, 2 * C), 1)
    ct = lax.broadcasted_iota(jnp.int32, (n, C, 2 * C), 2)
    incl = rt >= (ct & (C - 1))

    def seg_sum(x):
        s0 = jnp.sum(jnp.where(m0, x, 0.0), axis=-1, keepdims=True)
        s1 = jnp.sum(jnp.where(m0, 0.0, x), axis=-1, keepdims=True)
        return jnp.where(m0, s0, s1)

    def stack(x):
        return jnp.concatenate([jnp.where(m0, x, 0.0), jnp.where(m0, 0.0, x)], axis=1)

    r, k, v, lw, a = r_ref[...], k_ref[...], v_ref[...], lw_ref[...], a_ref[...]
    kkraw = kk_ref[...]
    kk = kkraw * lax.rsqrt(seg_sum(kkraw * kkraw) + L2_EPS)
    bv = kk * a
    lw_hi = lw.astype(BF16)
    lw_lo = (lw - lw_hi.astype(F32)).astype(BF16)
    cum2 = jnp.dot(tril_b, jnp.concatenate([lw_hi, lw_lo], axis=1), preferred_element_type=F32)
    cum = cum2[:, :P] + cum2[:, P:]
    w_inc = jnp.exp(cum)
    w_inv = jnp.exp(-cum)
    c3 = lambda t: t.reshape(n, C, P)
    w_end = c3(w_inc)[:, C - 1:C, :]
    r_t = c3(r * w_inc)
    a_st = stack(c3(-kk * jnp.exp(cum - lw)))
    b_st = stack(c3(bv * w_inv))
    k_st = stack(c3(k * w_inv))
    v_st = stack(c3(v))
    aa = _bmm_nt(jnp.concatenate([a_st, r_t], axis=1), jnp.concatenate([b_st, k_st], axis=1))
    a_ab = jnp.where(strict2, aa[:, :2 * C, :2 * C], 0.0)
    a_ak = jnp.where(strict2, aa[:, :2 * C, 2 * C:], 0.0)
    a_rb = jnp.where(incl, aa[:, 2 * C:, :2 * C], 0.0)
    a_rk = jnp.where(incl, aa[:, 2 * C:, 2 * C:], 0.0)
    pw = a_ab
    tinv = eye2 + pw
    for _ in range(5):
        pw = _bmm(pw, pw)
        tinv = tinv + _bmm(tinv, pw)
    wu = _bmm(tinv, jnp.concatenate([a_st, _bmm(a_ak, v_st)], axis=2))
    rw = _bmm(a_rb, wu)
    r_eff = r_t + rw[:, :, :P]
    o0 = rw[:, :, P:] + _bmm(a_rk, v_st)
    mz = _bmm_tn(wu, b_st * w_end)
    z = mz[:, P:, :] + _bmm_tn(v_st, k_st * w_end)
    S = S_ref[...]
    outs = []
    for c in range(n):
        outs.append(_mm_nt(r_eff[c], S) + o0[c])
        S = S * w_end[c] + _mm(S, mz[c, :P, :]) + z[c]
    S_ref[...] = S
    o = jnp.concatenate(outs, axis=0)
    mean = seg_sum(o) * (1.0 / C_DIM)
    d = o - mean
    var = seg_sum(d * d) * (1.0 / C_DIM)
    on = d * lax.rsqrt(var + C_GN_EPS) * gnw_ref[...] + gnb_ref[...]
    bonus = seg_sum(r * k * rk_ref[...]) * v
    o_ref[...] = (on + bonus) * g_ref[...]


def _rwkv_rec(r, k, v, lw, kk, a, g, r_k, gn_w, gn_b, Bn, S, tS=512):
    T = r.shape[0]
    nS = S // tS
    P = 2 * C_DIM
    blk = pl.BlockSpec((tS, P), lambda b, h, s: (b * nS + s, h))
    par = pl.BlockSpec((1, P), lambda b, h, s: (0, h))
    return pl.pallas_call(
        functools.partial(_rwkv_rec_body, tS=tS),
        grid=(Bn, C_WIDTH // P, nS),
        in_specs=[blk] * 7 + [par] * 3,
        out_specs=blk,
        out_shape=jax.ShapeDtypeStruct((T, C_WIDTH), F32),
        scratch_shapes=[pltpu.VMEM((P, P), F32)],
        compiler_params=_cparams(("parallel", "parallel", "arbitrary")),
        name="rwkv_rec",
    )(r, k, v, lw, kk, a, g, r_k, gn_w, gn_b)


def _merge_body(x_ref, o0_ref, o1_ref, o2_ref, l0_ref, l1_ref, l2_ref, yb_ref, yc_ref,
                ga_ref, gb_ref, gc_ref, pa_ref, pb_ref, pc_ref, wo_ref, o_ref, ya_s, yb_s, yc_s):
    @pl.when(pl.program_id(1) == 0)
    def _():
        l0, l1, l2 = l0_ref[...], l1_ref[...], l2_ref[...]
        m = jnp.maximum(jnp.maximum(l0, l1), l2)
        e0, e1, e2 = jnp.exp(l0 - m), jnp.exp(l1 - m), jnp.exp(l2 - m)
        ya = (e0 * o0_ref[...] + e1 * o1_ref[...] + e2 * o2_ref[...]) / (e0 + e1 + e2)
        ya_s[...] = ya.astype(BF16)
        yb_s[...] = yb_ref[...].astype(BF16)
        yc_s[...] = yc_ref[...].astype(BF16)
        o_ref[...] = x_ref[...]

    merged = (jax.nn.sigmoid(ga_ref[...]) * jnp.dot(ya_s[...], pa_ref[...], preferred_element_type=F32)
              + jax.nn.sigmoid(gb_ref[...]) * jnp.dot(yb_s[...], pb_ref[...], preferred_element_type=F32)
              + jax.nn.sigmoid(gc_ref[...]) * jnp.dot(yc_s[...], pc_ref[...], preferred_element_type=F32))
    o_ref[...] += jnp.dot(merged.astype(BF16), wo_ref[...], preferred_element_type=F32)


def _merge(x, o_g, lse_g, y_b, y_c, cols, proj_a, proj_b, proj_c, w_out, l, tm=512, tn=512):
    T, D = x.shape
    nj = D // tn

    def rows(width):
        return pl.BlockSpec((tm, width), lambda i, j: (i, 0))

    def gate(k):
        return pl.BlockSpec((tm, tn), lambda i, j: (i, P_G // tn + k * nj + j))

    def wcol(kdim):
        return pl.BlockSpec((None, kdim, tn), lambda i, j: (l, 0, j))

    return pl.pallas_call(
        _merge_body,
        grid=(T // tm, nj),
        in_specs=[rows(D)] + [rows(A_OUT)] * 6 + [rows(B_WIDTH), rows(C_WIDTH),
                  gate(0), gate(1), gate(2), wcol(A_OUT), wcol(B_WIDTH), wcol(C_WIDTH),
                  pl.BlockSpec((None, tn, D), lambda i, j: (l, j, 0))],
        out_specs=rows(D),
        out_shape=jax.ShapeDtypeStruct((T, D), F32),
        scratch_shapes=[pltpu.VMEM((tm, A_OUT), BF16), pltpu.VMEM((tm, B_WIDTH), BF16),
                        pltpu.VMEM((tm, C_WIDTH), BF16)],
        compiler_params=_cparams(("parallel", "arbitrary")),
        name="merge",
    )(x, *o_g, *lse_g, y_b, y_c, cols, cols, cols, proj_a, proj_b, proj_c, w_out)


def _norm_body(x_ref, g_ref, o_ref):
    o_ref[...] = _rms_rows(x_ref[...], g_ref[...])


def _final_norm(x, g, tm=1024):
    T, D = x.shape
    return pl.pallas_call(
        _norm_body,
        grid=(T // tm,),
        in_specs=[pl.BlockSpec((tm, D), lambda i: (i, 0)), pl.BlockSpec((1, D), lambda i: (0, 0))],
        out_specs=pl.BlockSpec((tm, D), lambda i: (i, 0)),
        out_shape=jax.ShapeDtypeStruct((T, D), F32),
        compiler_params=_cparams(("parallel",)),
        name="final_norm",
    )(x, g.reshape(1, D))


def _cast_body(w_ref, o_ref):
    o_ref[...] = w_ref[...].astype(BF16)


def _to_bf16(w):
    L, R, Cw = w.shape
    tr = 128 if Cw > 4096 else 256
    spec = pl.BlockSpec((None, tr, Cw), lambda l, i: (l, i, 0))
    return pl.pallas_call(
        _cast_body,
        grid=(L, R // tr),
        in_specs=[spec],
        out_specs=spec,
        out_shape=jax.ShapeDtypeStruct(w.shape, BF16),
        compiler_params=_cparams(("parallel", "parallel")),
        name="cast_bf16",
    )(w)


def _pack_w_in(w):
    a_cols = 3 * A_HEADS * HEAD_A
    b0 = a_cols
    b_ab = b0 + 4 * B_WIDTH
    c0 = b_ab + 2 * B_HEADS
    c_xw = c0 + 3 * C_WIDTH
    c_xa = c_xw + C_DECAY_LORA
    c_xg = c_xa + C_ICLR_LORA
    g0 = c_xg + C_GATE_LORA
    z = lambda n: jnp.zeros((w.shape[0], n), w.dtype)
    return jnp.concatenate([
        w[:, b0:b_ab], w[:, c0:c_xw], w[:, 0:a_cols], w[:, g0:],
        w[:, c_xw:c_xa], w[:, b_ab:c0], z(16), w[:, c_xa:c_xg], z(32), w[:, c_xg:g0],
    ], axis=1).astype(BF16)


def _pad_rows(w, n):
    return jnp.concatenate([w, jnp.zeros((n - w.shape[0], w.shape[1]), w.dtype)], axis=0)


def _mixers(x, cols, l, Bn, S, b_conv, b_a_log, b_dt_bias, b_norm, c_mu, c_w0, c_w2, c_a0, c_a2, c_g2,
            c_k_k, c_k_a, c_r_k, c_gn_w, c_gn_b):
    T = Bn * S
    o_g, lse_g = [], []
    for gi, d in enumerate(A_DILATIONS):
        L = S // d

        def to_res(off):
            t = cols[:, off + gi * A_OUT: off + (gi + 1) * A_OUT].reshape(Bn, L, d, A_OUT)
            return t.transpose(0, 2, 1, 3).reshape(Bn * d, L, A_OUT).astype(BF16)

        def from_res(t):
            return t.reshape(Bn, d, L, A_OUT).transpose(0, 2, 1, 3).reshape(T, A_OUT)

        o, lse = _attn_group(to_res(P_AQ), to_res(P_AK), to_res(P_AV), gi)
        o_g.append(from_res(o))
        lse_g.append(from_res(lse))
    y_b = _gdn(cols, b_conv[l], b_a_log[l], b_dt_bias[l], b_norm[l], Bn, S)
    W = C_WIDTH
    mu = c_mu[l]
    z = lambda n: jnp.zeros((n,), F32)
    mu_sm = jnp.concatenate([mu[3 * W:3 * W + C_DECAY_LORA], z(32),
                             mu[3 * W + C_DECAY_LORA:3 * W + C_DECAY_LORA + C_ICLR_LORA], z(32),
                             mu[3 * W + C_DECAY_LORA + C_ICLR_LORA:]])
    row = lambda t: t.reshape(1, -1)
    prep = _rwkv_prep(cols, row(mu[0:W]), row(mu[W:2 * W]), row(mu[2 * W:3 * W]), row(mu_sm),
                      row(c_w0[l]), _pad_rows(c_w2[l], 128).astype(BF16), row(c_a0[l]),
                      _pad_rows(c_a2[l], 128).astype(BF16), c_g2[l].astype(BF16),
                      row(c_k_k[l]), row(c_k_a[l]), Bn, S)
    y_c = _rwkv_rec(*prep, row(c_r_k[l]), row(c_gn_w[l]), row(c_gn_b[l]), Bn, S)
    return o_g, lse_g, y_b, y_c


def kernel(x, ffn1_norm, ffn1_w_gu, ffn1_w_down, mix_norm, w_in, b_conv, b_a_log, b_dt_bias, b_norm,
           c_mu, c_w0, c_w2, c_a0, c_a2, c_g2, c_k_k, c_k_a, c_r_k, c_gn_w, c_gn_b,
           proj_a, proj_b, proj_c, w_out, ffn2_norm, ffn2_w_gu, ffn2_w_down, final_norm):
    Bn, S, D = x.shape
    depth = w_in.shape[0]
    x = x.reshape(Bn * S, D)
    f1_gu, f1_dn, f2_gu, f2_dn = (_to_bf16(w) for w in (ffn1_w_gu, ffn1_w_down, ffn2_w_gu, ffn2_w_down))
    p_a, p_b, p_c, w_o = (_to_bf16(w) for w in (proj_a, proj_b, proj_c, w_out))
    for l in range(depth):
        x = _ffn(x, ffn1_norm[l], f1_gu, f1_dn, l)
        cols = _inproj(x, mix_norm[l], _pack_w_in(w_in[l]))
        o_g, lse_g, y_b, y_c = _mixers(x, cols, l, Bn, S, b_conv, b_a_log, b_dt_bias, b_norm, c_mu, c_w0,
                                       c_w2, c_a0, c_a2, c_g2, c_k_k, c_k_a, c_r_k, c_gn_w, c_gn_b)
        x = _merge(x, o_g, lse_g, y_b, y_c, cols, p_a, p_b, p_c, w_o, l)
        x = _ffn(x, ffn2_norm[l], f2_gu, f2_dn, l)
    return _final_norm(x, final_norm).reshape(Bn, S, D)
```

```python
import functools

import jax
import jax.numpy as jnp
from jax import lax
from jax.experimental import pallas as pl
from jax.experimental.pallas import tpu as pltpu

F32 = jnp.float32
BF16 = jnp.bfloat16

D_MODEL = 2048
D_FF = 5632
A_DILATIONS = (1, 4, 16)
A_HPG = 4
A_HEADS = 12
HEAD_A = 128
A_OUT = A_HPG * HEAD_A
A_STEPS = 128
ALIBI_MAX_BIAS = 8.0
B_HEADS = 8
B_DIM = 128
B_WIDTH = B_HEADS * B_DIM
B_CONV = 4
C_DIM = 64
C_HEADS = 16
C_WIDTH = C_HEADS * C_DIM
C_DECAY_LORA = 96
C_ICLR_LORA = 96
C_GATE_LORA = 256
C_GN_EPS = 64e-5
NORM_EPS = 1e-6
L2_EPS = 1e-6
NEG_INF = -1e30
CHUNK = 64

P_BQ, P_BK, P_BV, P_BZ = 0, 1024, 2048, 3072
P_CR, P_CK, P_CV = 4096, 5120, 6144
P_AQ, P_AK, P_AV = 7168, 8704, 10240
P_G = 11776
P_SM = 17920
P_COLS = 18432
SM_W = 512
SM_ALPHA, SM_BETA, SM_XA, SM_XG = 96, 104, 128, 256

VMEM_LIMIT = 56 * 1024 * 1024


def _cparams(sem):
    return pltpu.CompilerParams(dimension_semantics=sem, vmem_limit_bytes=VMEM_LIMIT)


def _mm(a, b):
    return jnp.dot(a.astype(BF16), b.astype(BF16), preferred_element_type=F32)


def _mm_nt(a, b):
    return lax.dot_general(a.astype(BF16), b.astype(BF16), (((1,), (1,)), ((), ())),
                           preferred_element_type=F32)


def _mm_tn(a, b):
    return lax.dot_general(a.astype(BF16), b.astype(BF16), (((0,), (0,)), ((), ())),
                           preferred_element_type=F32)


def _bmm(a, b):
    return jnp.einsum("bmk,bkn->bmn", a.astype(BF16), b.astype(BF16), preferred_element_type=F32)


def _bmm_nt(a, b):
    return jnp.einsum("bmk,bnk->bmn", a.astype(BF16), b.astype(BF16), preferred_element_type=F32)


def _bmm_tn(a, b):
    return jnp.einsum("bkm,bkn->bmn", a.astype(BF16), b.astype(BF16), preferred_element_type=F32)


def _silu(x):
    return x * jax.nn.sigmoid(x)


def _softplus(x):
    return jnp.maximum(x, 0.0) + jnp.log1p(jnp.exp(-jnp.abs(x)))


def _rms_rows(x, g):
    ms = jnp.mean(x * x, axis=-1, keepdims=True)
    return x * lax.rsqrt(ms + NORM_EPS) * g


def _ffn_body(x_ref, g_ref, wg_ref, wu_ref, wd_ref, o_ref, h_ref):
    @pl.when(pl.program_id(1) == 0)
    def _():
        x = x_ref[...]
        h_ref[...] = _rms_rows(x, g_ref[...]).astype(BF16)
        o_ref[...] = x

    h = h_ref[...]
    gate = jnp.dot(h, wg_ref[...], preferred_element_type=F32)
    up = jnp.dot(h, wu_ref[...], preferred_element_type=F32)
    act = (0.5 * _silu(gate) * up).astype(BF16)
    o_ref[...] += jnp.dot(act, wd_ref[...], preferred_element_type=F32)


def _ffn(x, g, w_gu, w_down, l, tm=512, tf=512):
    T, D = x.shape
    dff = w_down.shape[1]
    nj = dff // tf
    return pl.pallas_call(
        _ffn_body,
        grid=(T // tm, nj),
        in_specs=[
            pl.BlockSpec((tm, D), lambda i, j: (i, 0)),
            pl.BlockSpec((1, D), lambda i, j: (0, 0)),
            pl.BlockSpec((None, D, tf), lambda i, j: (l, 0, j)),
            pl.BlockSpec((None, D, tf), lambda i, j: (l, 0, j + nj)),
            pl.BlockSpec((None, tf, D), lambda i, j: (l, j, 0)),
        ],
        out_specs=pl.BlockSpec((tm, D), lambda i, j: (i, 0)),
        out_shape=jax.ShapeDtypeStruct((T, D), F32),
        scratch_shapes=[pltpu.VMEM((tm, D), BF16)],
        compiler_params=_cparams(("parallel", "arbitrary")),
        name="ffn",
    )(x, g.reshape(1, D), w_gu, w_gu, w_down)


def _inproj_body(x_ref, g_ref, w_ref, o_ref, h_ref):
    @pl.when(pl.program_id(1) == 0)
    def _():
        h_ref[...] = _rms_rows(x_ref[...], g_ref[...]).astype(BF16)

    o_ref[...] = jnp.dot(h_ref[...], w_ref[...], preferred_element_type=F32)


def _inproj(x, g, w, tm=1024, tn=1024):
    T, D = x.shape
    N = w.shape[1]
    return pl.pallas_call(
        _inproj_body,
        grid=(T // tm, N // tn),
        in_specs=[
            pl.BlockSpec((tm, D), lambda i, j: (i, 0)),
            pl.BlockSpec((1, D), lambda i, j: (0, 0)),
            pl.BlockSpec((D, tn), lambda i, j: (0, j)),
        ],
        out_specs=pl.BlockSpec((tm, tn), lambda i, j: (i, j)),
        out_shape=jax.ShapeDtypeStruct((T, N), F32),
        scratch_shapes=[pltpu.VMEM((tm, D), BF16)],
        compiler_params=_cparams(("parallel", "arbitrary")),
        name="inproj",
    )(x, g.reshape(1, D), w)


def _attn_body(slope_ref, *refs, S):
    n_g = len(A_DILATIONS)
    ins, o_ref, scr = refs[:3 * n_g], refs[3 * n_g], refs[3 * n_g + 1:]
    blk = A_STEPS
    slot = pl.program_id(1)
    qi = lax.broadcasted_iota(jnp.int32, (blk, 2 * blk), 0)
    ki = lax.broadcasted_iota(jnp.int32, (blk, 2 * blk), 1)
    delta2 = qi + blk - ki
    valid2 = (delta2 >= 0) & (delta2 <= A_STEPS)
    dist2 = delta2.astype(F32)
    delta1 = delta2[:, blk:]
    valid1 = delta1 >= 0
    dist1 = delta1.astype(F32)
    for g, d in enumerate(A_DILATIONS):
        q_ref, k_ref, v_ref = ins[3 * g:3 * g + 3]
        o_s, l_s = scr[2 * g], scr[2 * g + 1]
        slope = slope_ref[g * A_HPG + slot] * float(d)
        for r in range(d):
            k_prev = v_prev = None
            for j in range(S // (d * blk)):
                start = r + d * blk * j
                rows = pl.ds(start, blk, stride=d) if d > 1 else pl.ds(start, blk)
                q = q_ref[rows, :].astype(BF16)
                k_cur = k_ref[rows, :].astype(BF16)
                v_cur = v_ref[rows, :].astype(BF16)
                if j == 0:
                    k, v, valid, dist = k_cur, v_cur, valid1, dist1
                else:
                    k = jnp.concatenate([k_prev, k_cur], axis=0)
                    v = jnp.concatenate([v_prev, v_cur], axis=0)
                    valid, dist = valid2, dist2
                k_prev, v_prev = k_cur, v_cur
                s = _mm_nt(q, k) * (HEAD_A ** -0.5)
                s = jnp.where(valid, s - slope * dist, NEG_INF)
                m = jnp.max(s, axis=-1, keepdims=True)
                p = jnp.exp(s - m)
                l = jnp.sum(p, axis=-1, keepdims=True)
                o_s[rows, :] = _mm(p, v) / l
                l_s[rows, :] = jnp.broadcast_to(m + jnp.log(l), (blk, HEAD_A))
    l0, l1, l2 = scr[1][...], scr[3][...], scr[5][...]
    m = jnp.maximum(jnp.maximum(l0, l1), l2)
    e0, e1, e2 = jnp.exp(l0 - m), jnp.exp(l1 - m), jnp.exp(l2 - m)
    o_ref[...] = ((e0 * scr[0][...] + e1 * scr[2][...] + e2 * scr[4][...]) / (e0 + e1 + e2)).astype(o_ref.dtype)


def _attention(cols, Bn, S):
    T = cols.shape[0]
    slopes = jnp.asarray([2.0 ** (-ALIBI_MAX_BIAS * (h + 1.0) / A_HEADS) for h in range(A_HEADS)], F32)

    def col(off, g):
        return pl.BlockSpec((S, HEAD_A), lambda b, s: (b, off // HEAD_A + g * A_HPG + s))

    in_specs = [pl.BlockSpec(memory_space=pltpu.SMEM)]
    for g in range(len(A_DILATIONS)):
        in_specs += [col(P_AQ, g), col(P_AK, g), col(P_AV, g)]
    return pl.pallas_call(
        functools.partial(_attn_body, S=S),
        grid=(Bn, A_HPG),
        in_specs=in_specs,
        out_specs=pl.BlockSpec((S, HEAD_A), lambda b, s: (b, s)),
        out_shape=jax.ShapeDtypeStruct((T, A_OUT), BF16),
        scratch_shapes=[pltpu.VMEM((S, HEAD_A), F32)] * (2 * len(A_DILATIONS)),
        compiler_params=_cparams(("parallel", "parallel")),
        name="attention",
    )(slopes, *([cols] * (3 * len(A_DILATIONS))))


def _gdn_body(alog_ref, dtb_ref, q_ref, k_ref, v_ref, z_ref, sm_ref, cq_ref, ck_ref, cv_ref, nw_ref,
              o_ref, S_ref, bq, bk, bv, *, tS):
    C = CHUNK
    h = pl.program_id(1)
    s = pl.program_id(2)

    @pl.when(s == 0)
    def _():
        S_ref[...] = jnp.zeros_like(S_ref)
        for buf in (bq, bk, bv):
            buf[0:8, :] = jnp.zeros((8, B_DIM), F32)

    def conv_silu(raw_ref, buf, cw_ref):
        buf[8:8 + tS, :] = raw_ref[...]
        y = cw_ref[0:1, :] * buf[5:5 + tS, :]
        for j in range(1, B_CONV):
            y = y + cw_ref[j:j + 1, :] * buf[5 + j:5 + j + tS, :]
        buf[0:8, :] = buf[tS:tS + 8, :]
        return _silu(y)

    q_all = conv_silu(q_ref, bq, cq_ref)
    k_all = conv_silu(k_ref, bk, ck_ref)
    v_all = conv_silu(v_ref, bv, cv_ref)
    q_all = q_all * lax.rsqrt(jnp.sum(q_all * q_all, axis=-1, keepdims=True) + L2_EPS) * (B_DIM ** -0.5)
    k_all = k_all * lax.rsqrt(jnp.sum(k_all * k_all, axis=-1, keepdims=True) + L2_EPS)

    sm = sm_ref[:, 0:128]
    lane = lax.broadcasted_iota(jnp.int32, sm.shape, 1)
    alpha = jnp.sum(jnp.where(lane == SM_ALPHA + h, sm, 0.0), axis=-1, keepdims=True)
    beta_logit = jnp.sum(jnp.where(lane == SM_BETA + h, sm, 0.0), axis=-1, keepdims=True)
    a_neg = -jnp.exp(jnp.full((1, 1), alog_ref[h], F32))
    ld_all = a_neg * _softplus(alpha + dtb_ref[h])
    beta_all = jax.nn.sigmoid(beta_logit)

    n = tS // C
    ri = lax.broadcasted_iota(jnp.int32, (n, C, C), 1)
    ci = lax.broadcasted_iota(jnp.int32, (n, C, C), 2)
    tril = ri >= ci
    strict = ri > ci
    eye = ri == ci
    eye_f = jnp.where(eye, 1.0, 0.0).astype(F32)

    q = q_all.reshape(n, C, B_DIM)
    k = k_all.reshape(n, C, B_DIM)
    v = v_all.reshape(n, C, B_DIM)
    beta = beta_all.reshape(n, C, 1)
    LD = jnp.broadcast_to(ld_all.reshape(n, C, 1), (n, C, C))
    g_row = jnp.sum(jnp.where(ri <= ci, LD, 0.0), axis=1, keepdims=True)
    ld_row = jnp.sum(jnp.where(eye, LD, 0.0), axis=1, keepdims=True)
    g_col = jnp.sum(jnp.where(tril, jnp.broadcast_to(ld_row, (n, C, C)), 0.0), axis=2, keepdims=True)
    g_last = g_col[:, C - 1:C, :]
    decay = jnp.exp(jnp.where(tril, g_col - g_row, NEG_INF))
    eg = jnp.exp(g_col)
    kb = k * beta
    vb = v * beta
    kq = _bmm_nt(jnp.concatenate([kb, q], axis=1), k)
    m = jnp.where(strict, kq[:, :C] * decay, 0.0)
    att = jnp.where(tril, kq[:, C:] * decay, 0.0)
    pw = -m
    tinv = eye_f + pw
    for _ in range(5):
        pw = _bmm(pw, pw)
        tinv = tinv + _bmm(tinv, pw)
    wu = _bmm(tinv, jnp.concatenate([kb * eg, vb], axis=2))
    k_dec = k * jnp.exp(g_last - g_col)
    gz = _bmm_tn(k_dec, wu)
    aw = _bmm(att, wu)
    q_eff = q * eg - aw[:, :, :B_DIM]
    o0 = aw[:, :, B_DIM:]
    cd = jnp.exp(g_last)
    S = S_ref[...]
    outs = []
    for c in range(n):
        outs.append(_mm(q_eff[c], S) + o0[c])
        S = S * cd[c] - _mm(gz[c, :, :B_DIM], S) + gz[c, :, B_DIM:]
    S_ref[...] = S
    o = jnp.concatenate(outs, axis=0)
    o_ref[...] = (_rms_rows(o, nw_ref[...]) * _silu(z_ref[...])).astype(o_ref.dtype)


def _gdn(cols, b_conv, a_log, dt_bias, norm_w, Bn, S, tS=1024):
    T = cols.shape[0]
    nS = S // tS

    def col(off):
        return pl.BlockSpec((tS, B_DIM), lambda b, h, s: (b * nS + s, off // B_DIM + h))

    def cw(off):
        return pl.BlockSpec((B_CONV, B_DIM), lambda b, h, s: (0, off // B_DIM + h))

    smem = pl.BlockSpec(memory_space=pltpu.SMEM)
    return pl.pallas_call(
        functools.partial(_gdn_body, tS=tS),
        grid=(Bn, B_HEADS, nS),
        in_specs=[smem, smem, col(P_BQ), col(P_BK), col(P_BV), col(P_BZ),
                  pl.BlockSpec((tS, SM_W), lambda b, h, s: (b * nS + s, P_SM // SM_W)),
                  cw(0), cw(B_WIDTH), cw(2 * B_WIDTH),
                  pl.BlockSpec((1, B_DIM), lambda b, h, s: (0, 0))],
        out_specs=pl.BlockSpec((tS, B_DIM), lambda b, h, s: (b * nS + s, h)),
        out_shape=jax.ShapeDtypeStruct((T, B_WIDTH), BF16),
        scratch_shapes=[pltpu.VMEM((B_DIM, B_DIM), F32)] + [pltpu.VMEM((tS + 8, B_DIM), F32)] * 3,
        compiler_params=_cparams(("parallel", "parallel", "arbitrary")),
        name="gdn",
    )(a_log, dt_bias, cols, cols, cols, cols, cols, b_conv, b_conv, b_conv, norm_w.reshape(1, B_DIM))


def _rwkv_body(r_ref, k_ref, v_ref, sm_ref, mur_ref, muk_ref, muv_ref, musm_ref,
               w0_ref, w2_ref, a0_ref, a2_ref, g2_ref, kkw_ref, kaw_ref, rk_ref, gnw_ref, gnb_ref,
               o_ref, S_ref, br, bk, bv, bsm, *, tS):
    C = CHUNK
    P = 2 * C_DIM
    s = pl.program_id(2)

    @pl.when(s == 0)
    def _():
        S_ref[...] = jnp.zeros_like(S_ref)
        for buf in (br, bk, bv, bsm):
            buf[0:8, :] = jnp.zeros((8, buf.shape[1]), F32)

    def shift(x_ref, buf, mu_ref):
        x = x_ref[...]
        buf[8:8 + tS, :] = x
        prev = buf[7:7 + tS, :]
        buf[0:8, :] = buf[tS:tS + 8, :]
        return x + (prev - x) * mu_ref[...]

    r = shift(r_ref, br, mur_ref)
    k = shift(k_ref, bk, muk_ref)
    v = shift(v_ref, bv, muv_ref)
    sm = shift(sm_ref, bsm, musm_ref)
    xw, xa, xg = sm[:, 0:128], sm[:, SM_XA:SM_XA + 128], sm[:, SM_XG:SM_XG + C_GATE_LORA]
    w = -_softplus(-(w0_ref[...] + _mm(jnp.tanh(xw), w2_ref[...]))) - 0.5
    a = jax.nn.sigmoid(a0_ref[...] + _mm(xa, a2_ref[...]))
    lw = -jnp.exp(w)
    gate = _mm(jax.nn.sigmoid(xg), g2_ref[...])
    kkraw = k * kkw_ref[...]
    k = k * (1.0 + (a - 1.0) * kaw_ref[...])

    n = tS // C
    m0 = lax.broadcasted_iota(jnp.int32, (1, P), 1) < C_DIM
    rs = lax.broadcasted_iota(jnp.int32, (tS, tS), 0)
    cs = lax.broadcasted_iota(jnp.int32, (tS, tS), 1)
    tril_b = jnp.where((rs >= cs) & ((rs ^ cs) < C), 1.0, 0.0).astype(BF16)
    r2 = lax.broadcasted_iota(jnp.int32, (n, 2 * C, 2 * C), 1)
    c2 = lax.broadcasted_iota(jnp.int32, (n, 2 * C, 2 * C), 2)
    strict2 = r2 > c2
    eye2 = jnp.where(r2 == c2, 1.0, 0.0).astype(F32)
    rt = lax.broadcasted_iota(jnp.int32, (n, C, 2 * C), 1)
    ct = lax.broadcasted_iota(jnp.int32, (n, C, 2 * C), 2)
    incl = rt >= (ct & (C - 1))

    def seg_sum(x):
        s0 = jnp.sum(jnp.where(m0, x, 0.0), axis=-1, keepdims=True)
        s1 = jnp.sum(jnp.where(m0, 0.0, x), axis=-1, keepdims=True)
        return jnp.where(m0, s0, s1)

    def stack(x):
        return jnp.concatenate([jnp.where(m0, x, 0.0), jnp.where(m0, 0.0, x)], axis=1)

    kk = kkraw * lax.rsqrt(seg_sum(kkraw * kkraw) + L2_EPS)
    bv = kk * a
    lw_hi = lw.astype(BF16)
    lw_lo = (lw - lw_hi.astype(F32)).astype(BF16)
    cum2 = jnp.dot(tril_b, jnp.concatenate([lw_hi, lw_lo], axis=1), preferred_element_type=F32)
    cum = cum2[:, :P] + cum2[:, P:]
    w_inc = jnp.exp(cum)
    w_inv = jnp.exp(-cum)
    c3 = lambda t: t.reshape(n, C, P)
    w_end = c3(w_inc)[:, C - 1:C, :]
    r_t = c3(r * w_inc)
    a_st = stack(c3(-kk * jnp.exp(cum - lw)))
    b_st = stack(c3(bv * w_inv))
    k_st = stack(c3(k * w_inv))
    v_st = stack(c3(v))
    aa = _bmm_nt(jnp.concatenate([a_st, r_t], axis=1), jnp.concatenate([b_st, k_st], axis=1))
    a_ab = jnp.where(strict2, aa[:, :2 * C, :2 * C], 0.0)
    a_ak = jnp.where(strict2, aa[:, :2 * C, 2 * C:], 0.0)
    a_rb = jnp.where(incl, aa[:, 2 * C:, :2 * C], 0.0)
    a_rk = jnp.where(incl, aa[:, 2 * C:, 2 * C:], 0.0)
    pw = a_ab
    tinv = eye2 + pw
    for _ in range(5):
        pw = _bmm(pw, pw)
        tinv = tinv + _bmm(tinv, pw)
    wu = _bmm(tinv, jnp.concatenate([a_st, _bmm(a_ak, v_st)], axis=2))
    rw = _bmm(a_rb, wu)
    r_eff = r_t + rw[:, :, :P]
    o0 = rw[:, :, P:] + _bmm(a_rk, v_st)
    mz = _bmm_tn(wu, b_st * w_end)
    z = mz[:, P:, :] + _bmm_tn(v_st, k_st * w_end)
    S = S_ref[...]
    outs = []
    for c in range(n):
        outs.append(_mm_nt(r_eff[c], S) + o0[c])
        S = S * w_end[c] + _mm(S, mz[c, :P, :]) + z[c]
    S_ref[...] = S
    o = jnp.concatenate(outs, axis=0)
    mean = seg_sum(o) * (1.0 / C_DIM)
    d = o - mean
    var = seg_sum(d * d) * (1.0 / C_DIM)
    on = d * lax.rsqrt(var + C_GN_EPS) * gnw_ref[...] + gnb_ref[...]
    bonus = seg_sum(r * k * rk_ref[...]) * v
    o_ref[...] = ((on + bonus) * gate).astype(o_ref.dtype)


def _rwkv(cols, mu_r, mu_k, mu_v, mu_sm, w0, w2p, a0, a2p, g2, k_k, k_a, r_k, gn_w, gn_b, Bn, S, tS=512):
    T = cols.shape[0]
    nS = S // tS
    P = 2 * C_DIM

    def col(off):
        return pl.BlockSpec((tS, P), lambda b, h, s: (b * nS + s, off // P + h))

    def par(rows):
        return pl.BlockSpec((rows, P), lambda b, h, s: (0, h))

    sm = pl.BlockSpec((tS, SM_W), lambda b, h, s: (b * nS + s, P_SM // SM_W))
    return pl.pallas_call(
        functools.partial(_rwkv_body, tS=tS),
        grid=(Bn, C_WIDTH // P, nS),
        in_specs=[col(P_CR), col(P_CK), col(P_CV), sm, par(1), par(1), par(1),
                  pl.BlockSpec((1, SM_W), lambda b, h, s: (0, 0)),
                  par(1), par(128), par(1), par(128), par(C_GATE_LORA), par(1), par(1), par(1), par(1), par(1)],
        out_specs=pl.BlockSpec((tS, P), lambda b, h, s: (b * nS + s, h)),
        out_shape=jax.ShapeDtypeStruct((T, C_WIDTH), BF16),
        scratch_shapes=[pltpu.VMEM((P, P), F32)] + [pltpu.VMEM((tS + 8, P), F32)] * 3
                       + [pltpu.VMEM((tS + 8, SM_W), F32)],
        compiler_params=_cparams(("parallel", "parallel", "arbitrary")),
        name="rwkv",
    )(cols, cols, cols, cols, mu_r, mu_k, mu_v, mu_sm, w0, w2p, a0, a2p, g2, k_k, k_a, r_k, gn_w, gn_b)


def _merge_body(x_ref, ya_ref, yb_ref, yc_ref, ga_ref, gb_ref, gc_ref, pa_ref, pb_ref, pc_ref, wo_ref, o_ref):
    @pl.when(pl.program_id(1) == 0)
    def _():
        o_ref[...] = x_ref[...]

    merged = (jax.nn.sigmoid(ga_ref[...]) * jnp.dot(ya_ref[...], pa_ref[...], preferred_element_type=F32)
              + jax.nn.sigmoid(gb_ref[...]) * jnp.dot(yb_ref[...], pb_ref[...], preferred_element_type=F32)
              + jax.nn.sigmoid(gc_ref[...]) * jnp.dot(yc_ref[...], pc_ref[...], preferred_element_type=F32))
    o_ref[...] += jnp.dot(merged.astype(BF16), wo_ref[...], preferred_element_type=F32)


def _merge(x, y_a, y_b, y_c, cols, proj_a, proj_b, proj_c, w_out, l, tm=512, tn=512):
    T, D = x.shape
    nj = D // tn

    def rows(width):
        return pl.BlockSpec((tm, width), lambda i, j: (i, 0))

    def gate(k):
        return pl.BlockSpec((tm, tn), lambda i, j: (i, P_G // tn + k * nj + j))

    def wcol(kdim):
        return pl.BlockSpec((None, kdim, tn), lambda i, j: (l, 0, j))

    return pl.pallas_call(
        _merge_body,
        grid=(T // tm, nj),
        in_specs=[rows(D), rows(A_OUT), rows(B_WIDTH), rows(C_WIDTH),
                  gate(0), gate(1), gate(2), wcol(A_OUT), wcol(B_WIDTH), wcol(C_WIDTH),
                  pl.BlockSpec((None, tn, D), lambda i, j: (l, j, 0))],
        out_specs=rows(D),
        out_shape=jax.ShapeDtypeStruct((T, D), F32),
        compiler_params=_cparams(("parallel", "arbitrary")),
        name="merge",
    )(x, y_a, y_b, y_c, cols, cols, cols, proj_a, proj_b, proj_c, w_out)


def _norm_body(x_ref, g_ref, o_ref):
    o_ref[...] = _rms_rows(x_ref[...], g_ref[...])


def _final_norm(x, g, tm=1024):
    T, D = x.shape
    return pl.pallas_call(
        _norm_body,
        grid=(T // tm,),
        in_specs=[pl.BlockSpec((tm, D), lambda i: (i, 0)), pl.BlockSpec((1, D), lambda i: (0, 0))],
        out_specs=pl.BlockSpec((tm, D), lambda i: (i, 0)),
        out_shape=jax.ShapeDtypeStruct((T, D), F32),
        compiler_params=_cparams(("parallel",)),
        name="final_norm",
    )(x, g.reshape(1, D))


def _cast_body(w_ref, o_ref):
    o_ref[...] = w_ref[...].astype(BF16)


def _to_bf16(w):
    L, R, Cw = w.shape
    tr = 128 if Cw > 4096 else 256
    spec = pl.BlockSpec((None, tr, Cw), lambda l, i: (l, i, 0))
    return pl.pallas_call(
        _cast_body,
        grid=(L, R // tr),
        in_specs=[spec],
        out_specs=spec,
        out_shape=jax.ShapeDtypeStruct(w.shape, BF16),
        compiler_params=_cparams(("parallel", "parallel")),
        name="cast_bf16",
    )(w)


def _pack_w_in(w):
    a_cols = 3 * A_HEADS * HEAD_A
    b0 = a_cols
    b_ab = b0 + 4 * B_WIDTH
    c0 = b_ab + 2 * B_HEADS
    c_xw = c0 + 3 * C_WIDTH
    c_xa = c_xw + C_DECAY_LORA
    c_xg = c_xa + C_ICLR_LORA
    g0 = c_xg + C_GATE_LORA
    z = lambda n: jnp.zeros((w.shape[0], n), w.dtype)
    return jnp.concatenate([
        w[:, b0:b_ab], w[:, c0:c_xw], w[:, 0:a_cols], w[:, g0:],
        w[:, c_xw:c_xa], w[:, b_ab:c0], z(16), w[:, c_xa:c_xg], z(32), w[:, c_xg:g0],
    ], axis=1).astype(BF16)


def _pad_rows(w, n):
    return jnp.concatenate([w, jnp.zeros((n - w.shape[0], w.shape[1]), w.dtype)], axis=0)


def _mixers(x, cols, l, Bn, S, b_conv, b_a_log, b_dt_bias, b_norm, c_mu, c_w0, c_w2, c_a0, c_a2, c_g2,
            c_k_k, c_k_a, c_r_k, c_gn_w, c_gn_b):
    y_a = _attention(cols, Bn, S)
    y_b = _gdn(cols, b_conv[l], b_a_log[l], b_dt_bias[l], b_norm[l], Bn, S)
    W = C_WIDTH
    mu = c_mu[l]
    z = lambda n: jnp.zeros((n,), F32)
    mu_sm = jnp.concatenate([mu[3 * W:3 * W + C_DECAY_LORA], z(32),
                             mu[3 * W + C_DECAY_LORA:3 * W + C_DECAY_LORA + C_ICLR_LORA], z(32),
                             mu[3 * W + C_DECAY_LORA + C_ICLR_LORA:]])
    row = lambda t: t.reshape(1, -1)
    y_c = _rwkv(cols, row(mu[0:W]), row(mu[W:2 * W]), row(mu[2 * W:3 * W]), row(mu_sm),
                row(c_w0[l]), _pad_rows(c_w2[l], 128).astype(BF16), row(c_a0[l]),
                _pad_rows(c_a2[l], 128).astype(BF16), c_g2[l].astype(BF16),
                row(c_k_k[l]), row(c_k_a[l]), row(c_r_k[l]), row(c_gn_w[l]), row(c_gn_b[l]), Bn, S)
    return y_a, y_b, y_c


def kernel(x, ffn1_norm, ffn1_w_gu, ffn1_w_down, mix_norm, w_in, b_conv, b_a_log, b_dt_bias, b_norm,
           c_mu, c_w0, c_w2, c_a0, c_a2, c_g2, c_k_k, c_k_a, c_r_k, c_gn_w, c_gn_b,
           proj_a, proj_b, proj_c, w_out, ffn2_norm, ffn2_w_gu, ffn2_w_down, final_norm):
    Bn, S, D = x.shape
    depth = w_in.shape[0]
    x = x.reshape(Bn * S, D)
    f1_gu, f1_dn, f2_gu, f2_dn = (_to_bf16(w) for w in (ffn1_w_gu, ffn1_w_down, ffn2_w_gu, ffn2_w_down))
    p_a, p_b, p_c, w_o = (_to_bf16(w) for w in (proj_a, proj_b, proj_c, w_out))
    for l in range(depth):
        x = _ffn(x, ffn1_norm[l], f1_gu, f1_dn, l)
        cols = _inproj(x, mix_norm[l], _pack_w_in(w_in[l]))
        y_a, y_b, y_c = _mixers(x, cols, l, Bn, S, b_conv, b_a_log, b_dt_bias, b_norm, c_mu, c_w0,
                                c_w2, c_a0, c_a2, c_g2, c_k_k, c_k_a, c_r_k, c_gn_w, c_gn_b)
        x = _merge(x, y_a, y_b, y_c, cols, p_a, p_b, p_c, w_o, l)
        x = _ffn(x, ffn2_norm[l], f2_gu, f2_dn, l)
    return _final_norm(x, final_norm).reshape(Bn, S, D)
```

```python
import functools

import jax
import jax.numpy as jnp
from jax import lax
from jax.experimental import pallas as pl
from jax.experimental.pallas import tpu as pltpu

F32 = jnp.float32
BF16 = jnp.bfloat16

D_MODEL = 2048
D_FF = 5632
A_DILATIONS = (1, 4, 16)
A_HPG = 4
A_HEADS = 12
HEAD_A = 128
A_OUT = A_HPG * HEAD_A
A_STEPS = 128
ALIBI_MAX_BIAS = 8.0
B_HEADS = 8
B_DIM = 128
B_WIDTH = B_HEADS * B_DIM
B_CONV = 4
C_DIM = 64
C_HEADS = 16
C_WIDTH = C_HEADS * C_DIM
C_DECAY_LORA = 96
C_ICLR_LORA = 96
C_GATE_LORA = 256
C_GN_EPS = 64e-5
NORM_EPS = 1e-6
L2_EPS = 1e-6
NEG_INF = -1e30
CHUNK = 64
GROUP = 4 * CHUNK

P_BQ, P_BK, P_BV, P_BZ = 0, 1024, 2048, 3072
P_CR, P_CK, P_CV = 4096, 5120, 6144
P_AQ, P_AK, P_AV = 7168, 8704, 10240
P_G = 11776
P_SM = 17920
P_COLS = 18432
SM_W = 512
SM_ALPHA, SM_BETA, SM_XA, SM_XG = 96, 104, 128, 256

VMEM_LIMIT = 56 * 1024 * 1024


def _cparams(sem):
    return pltpu.CompilerParams(dimension_semantics=sem, vmem_limit_bytes=VMEM_LIMIT)


def _mm(a, b):
    return jnp.dot(a.astype(BF16), b.astype(BF16), preferred_element_type=F32)


def _mm_nt(a, b):
    return lax.dot_general(a.astype(BF16), b.astype(BF16), (((1,), (1,)), ((), ())),
                           preferred_element_type=F32)


def _mm_tn(a, b):
    return lax.dot_general(a.astype(BF16), b.astype(BF16), (((0,), (0,)), ((), ())),
                           preferred_element_type=F32)


def _bmm(a, b):
    return jnp.einsum("bmk,bkn->bmn", a.astype(BF16), b.astype(BF16), preferred_element_type=F32)


def _bmm_nt(a, b):
    return jnp.einsum("bmk,bnk->bmn", a.astype(BF16), b.astype(BF16), preferred_element_type=F32)


def _bmm_tn(a, b):
    return jnp.einsum("bkm,bkn->bmn", a.astype(BF16), b.astype(BF16), preferred_element_type=F32)


def _silu(x):
    return x * jax.nn.sigmoid(x)


def _softplus(x):
    return jnp.maximum(x, 0.0) + jnp.log1p(jnp.exp(-jnp.abs(x)))


def _rms_rows(x, g):
    ms = jnp.mean(x * x, axis=-1, keepdims=True)
    return x * lax.rsqrt(ms + NORM_EPS) * g


def _ffn_body(x_ref, g_ref, wg_ref, wu_ref, wd_ref, o_ref, h_ref):
    @pl.when(pl.program_id(1) == 0)
    def _():
        x = x_ref[...]
        h_ref[...] = _rms_rows(x, g_ref[...]).astype(BF16)
        o_ref[...] = x

    h = h_ref[...]
    gate = jnp.dot(h, wg_ref[...], preferred_element_type=F32)
    up = jnp.dot(h, wu_ref[...], preferred_element_type=F32)
    act = (0.5 * _silu(gate) * up).astype(BF16)
    o_ref[...] += jnp.dot(act, wd_ref[...], preferred_element_type=F32)


def _ffn(x, g, w_gu, w_down, l, tm=512, tf=512):
    T, D = x.shape
    dff = w_down.shape[1]
    nj = dff // tf
    return pl.pallas_call(
        _ffn_body,
        grid=(T // tm, nj),
        in_specs=[
            pl.BlockSpec((tm, D), lambda i, j: (i, 0)),
            pl.BlockSpec((1, D), lambda i, j: (0, 0)),
            pl.BlockSpec((None, D, tf), lambda i, j: (l, 0, j)),
            pl.BlockSpec((None, D, tf), lambda i, j: (l, 0, j + nj)),
            pl.BlockSpec((None, tf, D), lambda i, j: (l, j, 0)),
        ],
        out_specs=pl.BlockSpec((tm, D), lambda i, j: (i, 0)),
        out_shape=jax.ShapeDtypeStruct((T, D), F32),
        scratch_shapes=[pltpu.VMEM((tm, D), BF16)],
        compiler_params=_cparams(("parallel", "arbitrary")),
        name="ffn",
    )(x, g.reshape(1, D), w_gu, w_gu, w_down)


def _inproj_body(x_ref, g_ref, w_ref, o_ref, h_ref):
    @pl.when(pl.program_id(1) == 0)
    def _():
        h_ref[...] = _rms_rows(x_ref[...], g_ref[...]).astype(BF16)

    o_ref[...] = jnp.dot(h_ref[...], w_ref[...], preferred_element_type=F32)


def _inproj(x, g, w, tm=1024, tn=1024):
    T, D = x.shape
    N = w.shape[1]
    return pl.pallas_call(
        _inproj_body,
        grid=(T // tm, N // tn),
        in_specs=[
            pl.BlockSpec((tm, D), lambda i, j: (i, 0)),
            pl.BlockSpec((1, D), lambda i, j: (0, 0)),
            pl.BlockSpec((D, tn), lambda i, j: (0, j)),
        ],
        out_specs=pl.BlockSpec((tm, tn), lambda i, j: (i, j)),
        out_shape=jax.ShapeDtypeStruct((T, N), F32),
        scratch_shapes=[pltpu.VMEM((tm, D), BF16)],
        compiler_params=_cparams(("parallel", "arbitrary")),
        name="inproj",
    )(x, g.reshape(1, D), w)


def _attn_body(slope_ref, *refs, S):
    n_g = len(A_DILATIONS)
    ins, o_ref, scr = refs[:3 * n_g], refs[3 * n_g], refs[3 * n_g + 1:]
    blk = A_STEPS
    slot = pl.program_id(1)
    qi = lax.broadcasted_iota(jnp.int32, (blk, 2 * blk), 0)
    ki = lax.broadcasted_iota(jnp.int32, (blk, 2 * blk), 1)
    delta2 = qi + blk - ki
    valid2 = (delta2 >= 0) & (delta2 <= A_STEPS)
    dist2 = delta2.astype(F32)
    delta1 = delta2[:, blk:]
    valid1 = delta1 >= 0
    dist1 = delta1.astype(F32)
    for g, d in enumerate(A_DILATIONS):
        q_ref, k_ref, v_ref = ins[3 * g:3 * g + 3]
        o_s, l_s = scr[2 * g], scr[2 * g + 1]
        slope = slope_ref[g * A_HPG + slot] * float(d)
        for r in range(d):
            k_prev = v_prev = None
            for j in range(S // (d * blk)):
                start = r + d * blk * j
                rows = pl.ds(start, blk, stride=d) if d > 1 else pl.ds(start, blk)
                q = q_ref[rows, :].astype(BF16)
                k_cur = k_ref[rows, :].astype(BF16)
                v_cur = v_ref[rows, :].astype(BF16)
                if j == 0:
                    k, v, valid, dist = k_cur, v_cur, valid1, dist1
                else:
                    k = jnp.concatenate([k_prev, k_cur], axis=0)
                    v = jnp.concatenate([v_prev, v_cur], axis=0)
                    valid, dist = valid2, dist2
                k_prev, v_prev = k_cur, v_cur
                s = _mm_nt(q, k) * (HEAD_A ** -0.5)
                s = jnp.where(valid, s - slope * dist, NEG_INF)
                m = jnp.max(s, axis=-1, keepdims=True)
                p = jnp.exp(s - m)
                l = jnp.sum(p, axis=-1, keepdims=True)
                o_s[rows, :] = _mm(p, v) / l
                l_s[rows, :] = jnp.broadcast_to(m + jnp.log(l), (blk, HEAD_A))
    l0, l1, l2 = scr[1][...], scr[3][...], scr[5][...]
    m = jnp.maximum(jnp.maximum(l0, l1), l2)
    e0, e1, e2 = jnp.exp(l0 - m), jnp.exp(l1 - m), jnp.exp(l2 - m)
    o_ref[...] = ((e0 * scr[0][...] + e1 * scr[2][...] + e2 * scr[4][...]) / (e0 + e1 + e2)).astype(o_ref.dtype)


def _attention(cols, Bn, S):
    T = cols.shape[0]
    slopes = jnp.asarray([2.0 ** (-ALIBI_MAX_BIAS * (h + 1.0) / A_HEADS) for h in range(A_HEADS)], F32)

    def col(off, g):
        return pl.BlockSpec((S, HEAD_A), lambda b, s: (b, off // HEAD_A + g * A_HPG + s))

    in_specs = [pl.BlockSpec(memory_space=pltpu.SMEM)]
    for g in range(len(A_DILATIONS)):
        in_specs += [col(P_AQ, g), col(P_AK, g), col(P_AV, g)]
    return pl.pallas_call(
        functools.partial(_attn_body, S=S),
        grid=(Bn, A_HPG),
        in_specs=in_specs,
        out_specs=pl.BlockSpec((S, HEAD_A), lambda b, s: (b, s)),
        out_shape=jax.ShapeDtypeStruct((T, A_OUT), BF16),
        scratch_shapes=[pltpu.VMEM((S, HEAD_A), F32)] * (2 * len(A_DILATIONS)),
        compiler_params=_cparams(("parallel", "parallel")),
        name="attention",
    )(slopes, *([cols] * (3 * len(A_DILATIONS))))


def _group_masks(ng):
    ri = lax.broadcasted_iota(jnp.int32, (ng, GROUP, GROUP), 1)
    ci = lax.broadcasted_iota(jnp.int32, (ng, GROUP, GROUP), 2)
    same = (ri ^ ci) < CHUNK
    return same, same & (ri >= ci), same & (ri > ci)


def _neumann_inverse_bd(a, same):
    C = CHUNK
    ng = a.shape[0]
    spread = lambda x: jnp.where(same, jnp.concatenate([x] * (GROUP // C), axis=1), 0.0)
    ri = lax.broadcasted_iota(jnp.int32, (ng, C, GROUP), 1)
    ci = lax.broadcasted_iota(jnp.int32, (ng, C, GROUP), 2)
    pw = a[:, 0:C]
    for i in range(1, GROUP // C):
        pw = pw + a[:, i * C:(i + 1) * C]
    tinv = jnp.where(ri == (ci & (C - 1)), 1.0, 0.0) + pw
    pw = _bmm(pw, a)
    for _ in range(4):
        both = _bmm(jnp.concatenate([pw, tinv], axis=1), spread(pw))
        pw = both[:, :C]
        tinv = tinv + both[:, C:]
    tinv = tinv + _bmm(tinv, spread(pw))
    return spread(tinv)


def _gdn_body(alog_ref, dtb_ref, q_ref, k_ref, v_ref, z_ref, sm_ref, cw_ref, nw_ref,
              o_ref, S_ref, bq, bk, bv, *, tS):
    C, H, D, G = CHUNK, B_HEADS, B_DIM, GROUP
    n_c = tS // C
    nb = H * n_c
    ng = nb * C // G
    s = pl.program_id(1)

    @pl.when(s == 0)
    def _():
        S_ref[...] = jnp.zeros_like(S_ref)
        for buf in (bq, bk, bv):
            buf[0:8, :] = jnp.zeros((8, B_WIDTH), F32)

    def conv_silu(raw_ref, buf, off):
        buf[8:8 + tS, :] = raw_ref[...]
        y = cw_ref[0:1, off:off + B_WIDTH] * buf[5:5 + tS, :]
        for j in range(1, B_CONV):
            y = y + cw_ref[j:j + 1, off:off + B_WIDTH] * buf[5 + j:5 + j + tS, :]
        buf[0:8, :] = buf[tS:tS + 8, :]
        return _silu(y)

    def heads(x):
        return jnp.stack([x[:, h * D:(h + 1) * D] for h in range(H)], axis=0).reshape(nb, C, D)

    q = heads(conv_silu(q_ref, bq, 0))
    k = heads(conv_silu(k_ref, bk, B_WIDTH))
    v = heads(conv_silu(v_ref, bv, 2 * B_WIDTH))
    q = q * lax.rsqrt(jnp.sum(q * q, axis=-1, keepdims=True) + L2_EPS) * (D ** -0.5)
    k = k * lax.rsqrt(jnp.sum(k * k, axis=-1, keepdims=True) + L2_EPS)

    sm = sm_ref[:, 0:128]
    lane = lax.broadcasted_iota(jnp.int32, sm.shape, 1)
    ld, beta = [], []
    for h in range(H):
        alpha = jnp.sum(jnp.where(lane == SM_ALPHA + h, sm, 0.0), axis=-1, keepdims=True)
        beta_logit = jnp.sum(jnp.where(lane == SM_BETA + h, sm, 0.0), axis=-1, keepdims=True)
        a_neg = -jnp.exp(jnp.full((1, 1), alog_ref[h], F32))
        ld.append(a_neg * _softplus(alpha + dtb_ref[h]))
        beta.append(jax.nn.sigmoid(beta_logit))
    ld = jnp.stack(ld, axis=0).reshape(ng, G, 1)
    beta = jnp.stack(beta, axis=0).reshape(ng, G, 1)

    same, tril, strict = _group_masks(ng)
    ri = lax.broadcasted_iota(jnp.int32, (ng, G, G), 1)
    ci = lax.broadcasted_iota(jnp.int32, (ng, G, G), 2)
    grp = lambda t: t.reshape(ng, G, t.shape[-1])
    blk = lambda t: t.reshape(nb, C, t.shape[-1])
    q, k, v = grp(q), grp(k), grp(v)
    LD = jnp.broadcast_to(ld, (ng, G, G))
    g_row = jnp.sum(jnp.where(same & (ri <= ci), LD, 0.0), axis=1, keepdims=True)
    ld_row = jnp.sum(jnp.where(ri == ci, LD, 0.0), axis=1, keepdims=True)
    g_col = jnp.sum(jnp.where(tril, jnp.broadcast_to(ld_row, (ng, G, G)), 0.0), axis=2, keepdims=True)
    g_last = blk(g_col)[:, C - 1:C, :]
    decay = jnp.exp(jnp.where(tril, g_col - g_row, NEG_INF))
    eg = jnp.exp(g_col)
    kb = k * beta
    vb = v * beta
    kq = _bmm_nt(jnp.concatenate([kb, q], axis=1), k)
    m = jnp.where(strict, kq[:, :G] * decay, 0.0)
    att = jnp.where(tril, kq[:, G:] * decay, 0.0)
    tinv = _neumann_inverse_bd(-m, same)
    wu = _bmm(tinv, jnp.concatenate([kb * eg, vb], axis=2))
    k_dec = blk(k) * jnp.exp(g_last - blk(g_col))
    gz = _bmm_tn(k_dec, blk(wu))
    aw = _bmm(att, wu)
    q_eff = blk(q * eg - aw[:, :, :D])
    o0 = blk(aw[:, :, D:])
    cd = jnp.exp(g_last)
    S = [S_ref[h] for h in range(H)]
    outs = [[None] * n_c for _ in range(H)]
    for c in range(n_c):
        for h in range(H):
            i = h * n_c + c
            outs[h][c] = _mm(q_eff[i], S[h]) + o0[i]
            S[h] = S[h] * cd[i] - _mm(gz[i, :, :D], S[h]) + gz[i, :, D:]
    nw = nw_ref[...]
    for h in range(H):
        S_ref[h] = S[h]
        o = jnp.concatenate(outs[h], axis=0)
        hs = slice(h * D, (h + 1) * D)
        o_ref[:, hs] = (_rms_rows(o, nw) * _silu(z_ref[:, hs])).astype(o_ref.dtype)


def _gdn(cols, b_conv, a_log, dt_bias, norm_w, Bn, S, tS=128):
    T = cols.shape[0]
    nS = S // tS
    W = B_WIDTH

    def col(off):
        return pl.BlockSpec((tS, W), lambda b, s: (b * nS + s, off // W))

    smem = pl.BlockSpec(memory_space=pltpu.SMEM)
    return pl.pallas_call(
        functools.partial(_gdn_body, tS=tS),
        grid=(Bn, nS),
        in_specs=[smem, smem, col(P_BQ), col(P_BK), col(P_BV), col(P_BZ),
                  pl.BlockSpec((tS, SM_W), lambda b, s: (b * nS + s, P_SM // SM_W)),
                  pl.BlockSpec((B_CONV, 3 * W), lambda b, s: (0, 0)),
                  pl.BlockSpec((1, B_DIM), lambda b, s: (0, 0))],
        out_specs=pl.BlockSpec((tS, W), lambda b, s: (b * nS + s, 0)),
        out_shape=jax.ShapeDtypeStruct((T, W), BF16),
        scratch_shapes=[pltpu.VMEM((B_HEADS, B_DIM, B_DIM), F32)] + [pltpu.VMEM((tS + 8, W), F32)] * 3,
        compiler_params=_cparams(("parallel", "arbitrary")),
        name="gdn",
    )(a_log, dt_bias, cols, cols, cols, cols, cols, b_conv, norm_w.reshape(1, B_DIM))


def _rwkv_body(r_ref, k_ref, v_ref, sm_ref, mur_ref, muk_ref, muv_ref, musm_ref,
               w0_ref, w2_ref, a0_ref, a2_ref, g2_ref, kkw_ref, kaw_ref, rk_ref, gnw_ref, gnb_ref,
               o_ref, S_ref, br, bk, bv, bsm, *, tS):
    C = CHUNK
    P = 2 * C_DIM
    NP = C_WIDTH // P
    s = pl.program_id(1)

    @pl.when(s == 0)
    def _():
        S_ref[...] = jnp.zeros_like(S_ref)
        for buf in (br, bk, bv, bsm):
            buf[0:8, :] = jnp.zeros((8, buf.shape[1]), F32)

    def shift(x_ref, buf, mu_ref):
        x = x_ref[...]
        buf[8:8 + tS, :] = x
        prev = buf[7:7 + tS, :]
        buf[0:8, :] = buf[tS:tS + 8, :]
        return x + (prev - x) * mu_ref[...]

    r = shift(r_ref, br, mur_ref)
    k = shift(k_ref, bk, muk_ref)
    v = shift(v_ref, bv, muv_ref)
    sm = shift(sm_ref, bsm, musm_ref)
    xw, xa, xg = sm[:, 0:128], sm[:, SM_XA:SM_XA + 128], sm[:, SM_XG:SM_XG + C_GATE_LORA]
    w = -_softplus(-(w0_ref[...] + _mm(jnp.tanh(xw), w2_ref[...]))) - 0.5
    a = jax.nn.sigmoid(a0_ref[...] + _mm(xa, a2_ref[...]))
    lw = -jnp.exp(w)
    gate = _mm(jax.nn.sigmoid(xg), g2_ref[...])
    kkraw = k * kkw_ref[...]
    k = k * (1.0 + (a - 1.0) * kaw_ref[...])

    n_c = tS // C
    nb = NP * n_c
    ng = nb * 2 * C // GROUP
    m0 = lax.broadcasted_iota(jnp.int32, (1, P), 1) < C_DIM
    rs = lax.broadcasted_iota(jnp.int32, (tS, tS), 0)
    cs = lax.broadcasted_iota(jnp.int32, (tS, tS), 1)
    tril_b = jnp.where((rs >= cs) & ((rs ^ cs) < C), 1.0, 0.0).astype(BF16)
    r2 = lax.broadcasted_iota(jnp.int32, (1, 2 * C, 2 * C), 1)
    c2 = lax.broadcasted_iota(jnp.int32, (1, 2 * C, 2 * C), 2)
    strict2 = r2 > c2
    rt = lax.broadcasted_iota(jnp.int32, (1, C, 2 * C), 1)
    ct = lax.broadcasted_iota(jnp.int32, (1, C, 2 * C), 2)
    incl = rt >= (ct & (C - 1))

    def seg_sum(x):
        s0 = jnp.sum(jnp.where(m0, x, 0.0), axis=-1, keepdims=True)
        s1 = jnp.sum(jnp.where(m0, 0.0, x), axis=-1, keepdims=True)
        return jnp.where(m0, s0, s1)

    def pairs(x):
        return jnp.stack([x[:, p * P:(p + 1) * P] for p in range(NP)], axis=0).reshape(nb, C, P)

    def stack(x):
        return jnp.concatenate([jnp.where(m0, x, 0.0), jnp.where(m0, 0.0, x)], axis=1)

    lw_hi = lw.astype(BF16)
    lw_lo = (lw - lw_hi.astype(F32)).astype(BF16)
    cum2 = jnp.dot(tril_b, jnp.concatenate([lw_hi, lw_lo], axis=1), preferred_element_type=F32)
    cum = cum2[:, :C_WIDTH] + cum2[:, C_WIDTH:]
    kkraw = pairs(kkraw)
    kk = kkraw * lax.rsqrt(seg_sum(kkraw * kkraw) + L2_EPS)
    bv = kk * pairs(a)
    w_inc = pairs(jnp.exp(cum))
    w_inv = pairs(jnp.exp(-cum))
    w_end = w_inc[:, C - 1:C, :]
    r_t = pairs(r) * w_inc
    a_st = stack(-kk * pairs(jnp.exp(cum - lw)))
    b_st = stack(bv * w_inv)
    k_st = stack(pairs(k) * w_inv)
    v_st = stack(pairs(v))
    aa = _bmm_nt(jnp.concatenate([a_st, r_t], axis=1), jnp.concatenate([b_st, k_st], axis=1))
    a_ab = jnp.where(strict2, aa[:, :2 * C, :2 * C], 0.0)
    a_ak = jnp.where(strict2, aa[:, :2 * C, 2 * C:], 0.0)
    a_rb = jnp.where(incl, aa[:, 2 * C:, :2 * C], 0.0)
    a_rk = jnp.where(incl, aa[:, 2 * C:, 2 * C:], 0.0)
    a4 = a_ab.reshape(ng, 2, 2 * C, 2 * C)
    zero = jnp.zeros((ng, 2 * C, 2 * C), F32)
    a_bd = jnp.concatenate([jnp.concatenate([a4[:, 0], zero], axis=2),
                            jnp.concatenate([zero, a4[:, 1]], axis=2)], axis=1)
    tinv = _neumann_inverse_bd(a_bd, _group_masks(ng)[0])
    x = jnp.concatenate([a_st, _bmm(a_ak, v_st)], axis=2)
    wu = _bmm(tinv, x.reshape(ng, GROUP, 2 * P)).reshape(nb, 2 * C, 2 * P)
    rw = _bmm(a_rb, wu)
    r_eff = r_t + rw[:, :, :P]
    o0 = rw[:, :, P:] + _bmm(a_rk, v_st)
    mz = _bmm_tn(wu, b_st * w_end)
    z = mz[:, P:, :] + _bmm_tn(v_st, k_st * w_end)
    S = [S_ref[p] for p in range(NP)]
    outs = [[None] * n_c for _ in range(NP)]
    for c in range(n_c):
        for p in range(NP):
            i = p * n_c + c
            outs[p][c] = _mm_nt(r_eff[i], S[p]) + o0[i]
            S[p] = S[p] * w_end[i] + _mm(S[p], mz[i, :P, :]) + z[i]
    for p in range(NP):
        S_ref[p] = S[p]
        ps = slice(p * P, (p + 1) * P)
        o = jnp.concatenate(outs[p], axis=0)
        mean = seg_sum(o) * (1.0 / C_DIM)
        d = o - mean
        var = seg_sum(d * d) * (1.0 / C_DIM)
        on = d * lax.rsqrt(var + C_GN_EPS) * gnw_ref[:, ps] + gnb_ref[:, ps]
        bonus = seg_sum(r[:, ps] * k[:, ps] * rk_ref[:, ps]) * v[:, ps]
        o_ref[:, ps] = ((on + bonus) * gate[:, ps]).astype(o_ref.dtype)


def _rwkv(cols, mu_r, mu_k, mu_v, mu_sm, w0, w2p, a0, a2p, g2, k_k, k_a, r_k, gn_w, gn_b, Bn, S, tS=128):
    T = cols.shape[0]
    nS = S // tS
    W = C_WIDTH
    P = 2 * C_DIM

    def col(off, width):
        return pl.BlockSpec((tS, width), lambda b, s: (b * nS + s, off // width))

    def par(rows, width=W):
        return pl.BlockSpec((rows, width), lambda b, s: (0, 0))

    return pl.pallas_call(
        functools.partial(_rwkv_body, tS=tS),
        grid=(Bn, nS),
        in_specs=[col(P_CR, W), col(P_CK, W), col(P_CV, W), col(P_SM, SM_W), par(1), par(1), par(1),
                  par(1, SM_W), par(1), par(128), par(1), par(128), par(C_GATE_LORA), par(1), par(1),
                  par(1), par(1), par(1)],
        out_specs=pl.BlockSpec((tS, W), lambda b, s: (b * nS + s, 0)),
        out_shape=jax.ShapeDtypeStruct((T, W), BF16),
        scratch_shapes=[pltpu.VMEM((W // P, P, P), F32)] + [pltpu.VMEM((tS + 8, W), F32)] * 3
                       + [pltpu.VMEM((tS + 8, SM_W), F32)],
        compiler_params=_cparams(("parallel", "arbitrary")),
        name="rwkv",
    )(cols, cols, cols, cols, mu_r, mu_k, mu_v, mu_sm, w0, w2p, a0, a2p, g2, k_k, k_a, r_k, gn_w, gn_b)


def _merge_body(x_ref, ya_ref, yb_ref, yc_ref, ga_ref, gb_ref, gc_ref, pa_ref, pb_ref, pc_ref, wo_ref, o_ref):
    @pl.when(pl.program_id(1) == 0)
    def _():
        o_ref[...] = x_ref[...]

    merged = (jax.nn.sigmoid(ga_ref[...]) * jnp.dot(ya_ref[...], pa_ref[...], preferred_element_type=F32)
              + jax.nn.sigmoid(gb_ref[...]) * jnp.dot(yb_ref[...], pb_ref[...], preferred_element_type=F32)
              + jax.nn.sigmoid(gc_ref[...]) * jnp.dot(yc_ref[...], pc_ref[...], preferred_element_type=F32))
    o_ref[...] += jnp.dot(merged.astype(BF16), wo_ref[...], preferred_element_type=F32)


def _merge(x, y_a, y_b, y_c, cols, proj_a, proj_b, proj_c, w_out, l, tm=512, tn=512):
    T, D = x.shape
    nj = D // tn

    def rows(width):
        return pl.BlockSpec((tm, width), lambda i, j: (i, 0))

    def gate(k):
        return pl.BlockSpec((tm, tn), lambda i, j: (i, P_G // tn + k * nj + j))

    def wcol(kdim):
        return pl.BlockSpec((None, kdim, tn), lambda i, j: (l, 0, j))

    return pl.pallas_call(
        _merge_body,
        grid=(T // tm, nj),
        in_specs=[rows(D), rows(A_OUT), rows(B_WIDTH), rows(C_WIDTH),
                  gate(0), gate(1), gate(2), wcol(A_OUT), wcol(B_WIDTH), wcol(C_WIDTH),
                  pl.BlockSpec((None, tn, D), lambda i, j: (l, j, 0))],
        out_specs=rows(D),
        out_shape=jax.ShapeDtypeStruct((T, D), F32),
        compiler_params=_cparams(("parallel", "arbitrary")),
        name="merge",
    )(x, y_a, y_b, y_c, cols, cols, cols, proj_a, proj_b, proj_c, w_out)


def _norm_body(x_ref, g_ref, o_ref):
    o_ref[...] = _rms_rows(x_ref[...], g_ref[...])


def _final_norm(x, g, tm=1024):
    T, D = x.shape
    return pl.pallas_call(
        _norm_body,
        grid=(T // tm,),
        in_specs=[pl.BlockSpec((tm, D), lambda i: (i, 0)), pl.BlockSpec((1, D), lambda i: (0, 0))],
        out_specs=pl.BlockSpec((tm, D), lambda i: (i, 0)),
        out_shape=jax.ShapeDtypeStruct((T, D), F32),
        compiler_params=_cparams(("parallel",)),
        name="final_norm",
    )(x, g.reshape(1, D))


def _cast_body(w_ref, o_ref):
    o_ref[...] = w_ref[...].astype(BF16)


def _to_bf16(w):
    L, R, Cw = w.shape
    tr = 128 if Cw > 4096 else 256
    spec = pl.BlockSpec((None, tr, Cw), lambda l, i: (l, i, 0))
    return pl.pallas_call(
        _cast_body,
        grid=(L, R // tr),
        in_specs=[spec],
        out_specs=spec,
        out_shape=jax.ShapeDtypeStruct(w.shape, BF16),
        compiler_params=_cparams(("parallel", "parallel")),
        name="cast_bf16",
    )(w)


def _pack_w_in(w):
    a_cols = 3 * A_HEADS * HEAD_A
    b0 = a_cols
    b_ab = b0 + 4 * B_WIDTH
    c0 = b_ab + 2 * B_HEADS
    c_xw = c0 + 3 * C_WIDTH
    c_xa = c_xw + C_DECAY_LORA
    c_xg = c_xa + C_ICLR_LORA
    g0 = c_xg + C_GATE_LORA
    z = lambda n: jnp.zeros((w.shape[0], n), w.dtype)
    return jnp.concatenate([
        w[:, b0:b_ab], w[:, c0:c_xw], w[:, 0:a_cols], w[:, g0:],
        w[:, c_xw:c_xa], w[:, b_ab:c0], z(16), w[:, c_xa:c_xg], z(32), w[:, c_xg:g0],
    ], axis=1).astype(BF16)


def _pad_rows(w, n):
    return jnp.concatenate([w, jnp.zeros((n - w.shape[0], w.shape[1]), w.dtype)], axis=0)


def _mixers(x, cols, l, Bn, S, b_conv, b_a_log, b_dt_bias, b_norm, c_mu, c_w0, c_w2, c_a0, c_a2, c_g2,
            c_k_k, c_k_a, c_r_k, c_gn_w, c_gn_b):
    y_a = _attention(cols, Bn, S)
    y_b = _gdn(cols, b_conv[l], b_a_log[l], b_dt_bias[l], b_norm[l], Bn, S)
    W = C_WIDTH
    mu = c_mu[l]
    z = lambda n: jnp.zeros((n,), F32)
    mu_sm = jnp.concatenate([mu[3 * W:3 * W + C_DECAY_LORA], z(32),
                             mu[3 * W + C_DECAY_LORA:3 * W + C_DECAY_LORA + C_ICLR_LORA], z(32),
                             mu[3 * W + C_DECAY_LORA + C_ICLR_LORA:]])
    row = lambda t: t.reshape(1, -1)
    y_c = _rwkv(cols, row(mu[0:W]), row(mu[W:2 * W]), row(mu[2 * W:3 * W]), row(mu_sm),
                row(c_w0[l]), _pad_rows(c_w2[l], 128).astype(BF16), row(c_a0[l]),
                _pad_rows(c_a2[l], 128).astype(BF16), c_g2[l].astype(BF16),
                row(c_k_k[l]), row(c_k_a[l]), row(c_r_k[l]), row(c_gn_w[l]), row(c_gn_b[l]), Bn, S)
    return y_a, y_b, y_c


def kernel(x, ffn1_norm, ffn1_w_gu, ffn1_w_down, mix_norm, w_in, b_conv, b_a_log, b_dt_bias, b_norm,
           c_mu, c_w0, c_w2, c_a0, c_a2, c_g2, c_k_k, c_k_a, c_r_k, c_gn_w, c_gn_b,
           proj_a, proj_b, proj_c, w_out, ffn2_norm, ffn2_w_gu, ffn2_w_down, final_norm):
    Bn, S, D = x.shape
    depth = w_in.shape[0]
    x = x.reshape(Bn * S, D)
    f1_gu, f1_dn, f2_gu, f2_dn = (_to_bf16(w) for w in (ffn1_w_gu, ffn1_w_down, ffn2_w_gu, ffn2_w_down))
    p_a, p_b, p_c, w_o = (_to_bf16(w) for w in (proj_a, proj_b, proj_c, w_out))
    for l in range(depth):
        x = _ffn(x, ffn1_norm[l], f1_gu, f1_dn, l)
        cols = _inproj(x, mix_norm[l], _pack_w_in(w_in[l]))
        y_a, y_b, y_c = _mixers(x, cols, l, Bn, S, b_conv, b_a_log, b_dt_bias, b_norm, c_mu, c_w0,
                                c_w2, c_a0, c_a2, c_g2, c_k_k, c_k_a, c_r_k, c_gn_w, c_gn_b)
        x = _merge(x, y_a, y_b, y_c, cols, p_a, p_b, p_c, w_o, l)
        x = _ffn(x, ffn2_norm[l], f2_gu, f2_dn, l)
    return _final_norm(x, final_norm).reshape(Bn, S, D)
```

```python
import functools

import jax
import jax.numpy as jnp
from jax import lax
from jax.experimental import pallas as pl
from jax.experimental.pallas import tpu as pltpu

F32 = jnp.float32
BF16 = jnp.bfloat16

D_MODEL = 2048
D_FF = 5632
A_DILATIONS = (1, 4, 16)
A_HPG = 4
A_HEADS = 12
HEAD_A = 128
A_OUT = A_HPG * HEAD_A
A_STEPS = 128
ALIBI_MAX_BIAS = 8.0
B_HEADS = 8
B_DIM = 128
B_WIDTH = B_HEADS * B_DIM
B_CONV = 4
C_DIM = 64
C_HEADS = 16
C_WIDTH = C_HEADS * C_DIM
C_DECAY_LORA = 96
C_ICLR_LORA = 96
C_GATE_LORA = 256
C_GN_EPS = 64e-5
NORM_EPS = 1e-6
L2_EPS = 1e-6
NEG_INF = -1e30
CHUNK = 64
GROUP = 4 * CHUNK

P_G = 0
P_BQ, P_BK, P_BV, P_BZ = 6144, 7168, 8192, 9216
P_CR, P_CK, P_CV = 10240, 11264, 12288
P_AQ, P_AK, P_AV = 13312, 14848, 16384
P_SM = 17920
P_COLS = 18432
SM_W = 512
SM_ALPHA, SM_BETA, SM_XA, SM_XG = 96, 104, 128, 256

VMEM_LIMIT = 56 * 1024 * 1024


def _cparams(sem):
    return pltpu.CompilerParams(dimension_semantics=sem, vmem_limit_bytes=VMEM_LIMIT)


def _mm(a, b):
    return jnp.dot(a.astype(BF16), b.astype(BF16), preferred_element_type=F32)


def _mm_nt(a, b):
    return lax.dot_general(a.astype(BF16), b.astype(BF16), (((1,), (1,)), ((), ())),
                           preferred_element_type=F32)


def _mm_tn(a, b):
    return lax.dot_general(a.astype(BF16), b.astype(BF16), (((0,), (0,)), ((), ())),
                           preferred_element_type=F32)


def _bmm(a, b):
    return jnp.einsum("bmk,bkn->bmn", a.astype(BF16), b.astype(BF16), preferred_element_type=F32)


def _bmm_nt(a, b):
    return jnp.einsum("bmk,bnk->bmn", a.astype(BF16), b.astype(BF16), preferred_element_type=F32)


def _bmm_tn(a, b):
    return jnp.einsum("bkm,bkn->bmn", a.astype(BF16), b.astype(BF16), preferred_element_type=F32)


def _silu(x):
    return x * jax.nn.sigmoid(x)


def _softplus(x):
    return jnp.maximum(x, 0.0) + jnp.log1p(jnp.exp(-jnp.abs(x)))


def _rms_rows(x, g):
    ms = jnp.mean(x * x, axis=-1, keepdims=True)
    return x * lax.rsqrt(ms + NORM_EPS) * g


def _ffn_body(x_ref, g_ref, wg_ref, wu_ref, wd_ref, *rest, with_out_norm):
    o_ref, h_ref = rest[-2], rest[-1]

    @pl.when(pl.program_id(1) == 0)
    def _():
        x = x_ref[...]
        h_ref[...] = _rms_rows(x, g_ref[...]).astype(BF16)
        o_ref[...] = x

    h = h_ref[...]
    gate = jnp.dot(h, wg_ref[...], preferred_element_type=F32)
    up = jnp.dot(h, wu_ref[...], preferred_element_type=F32)
    act = (0.5 * _silu(gate) * up).astype(BF16)
    o_ref[...] += jnp.dot(act, wd_ref[...], preferred_element_type=F32)

    if with_out_norm:
        @pl.when(pl.program_id(1) == pl.num_programs(1) - 1)
        def _():
            o_ref[...] = _rms_rows(o_ref[...], rest[0][...])


def _ffn(x, g, w_gu, w_down, l, out_norm=None, tm=1024, tf=512):
    T, D = x.shape
    dff = w_down.shape[1]
    nj = dff // tf
    vec = pl.BlockSpec((1, D), lambda i, j: (0, 0))
    extra = [] if out_norm is None else [out_norm.reshape(1, D)]
    return pl.pallas_call(
        functools.partial(_ffn_body, with_out_norm=out_norm is not None),
        grid=(T // tm, nj),
        in_specs=[
            pl.BlockSpec((tm, D), lambda i, j: (i, 0)),
            vec,
            pl.BlockSpec((None, D, tf), lambda i, j: (l, 0, j)),
            pl.BlockSpec((None, D, tf), lambda i, j: (l, 0, j + nj)),
            pl.BlockSpec((None, tf, D), lambda i, j: (l, j, 0)),
        ] + [vec] * len(extra),
        out_specs=pl.BlockSpec((tm, D), lambda i, j: (i, 0)),
        out_shape=jax.ShapeDtypeStruct((T, D), F32),
        scratch_shapes=[pltpu.VMEM((tm, D), BF16)],
        compiler_params=_cparams(("parallel", "arbitrary")),
        name="ffn",
    )(x, g.reshape(1, D), w_gu, w_gu, w_down, *extra)


def _inproj_body(x_ref, g_ref, w_ref, o_ref, h_ref):
    @pl.when(pl.program_id(1) == 0)
    def _():
        h_ref[...] = _rms_rows(x_ref[...], g_ref[...]).astype(BF16)

    o_ref[...] = jnp.dot(h_ref[...], w_ref[...], preferred_element_type=F32)


def _inproj(x, g, w, l, tm=1024, tn=1024):
    T, D = x.shape
    N = w.shape[2]
    return pl.pallas_call(
        _inproj_body,
        grid=(T // tm, N // tn),
        in_specs=[
            pl.BlockSpec((tm, D), lambda i, j: (i, 0)),
            pl.BlockSpec((1, D), lambda i, j: (0, 0)),
            pl.BlockSpec((None, D, tn), lambda i, j: (l, 0, j)),
        ],
        out_specs=pl.BlockSpec((tm, tn), lambda i, j: (i, j)),
        out_shape=jax.ShapeDtypeStruct((T, N), F32),
        scratch_shapes=[pltpu.VMEM((tm, D), BF16)],
        compiler_params=_cparams(("parallel", "arbitrary")),
        name="inproj",
    )(x, g.reshape(1, D), w)


def _attn_body(slope_ref, *refs, S):
    n_g = len(A_DILATIONS)
    ins, o_ref, scr = refs[:3 * n_g], refs[3 * n_g], refs[3 * n_g + 1:]
    blk = A_STEPS
    slot = pl.program_id(1)
    qi = lax.broadcasted_iota(jnp.int32, (blk, 2 * blk), 0)
    ki = lax.broadcasted_iota(jnp.int32, (blk, 2 * blk), 1)
    delta2 = qi + blk - ki
    valid2 = (delta2 >= 0) & (delta2 <= A_STEPS)
    dist2 = delta2.astype(F32)
    delta1 = delta2[:, blk:]
    valid1 = delta1 >= 0
    dist1 = delta1.astype(F32)
    for g, d in enumerate(A_DILATIONS):
        q_ref, k_ref, v_ref = ins[3 * g:3 * g + 3]
        o_s, l_s = scr[2 * g], scr[2 * g + 1]
        slope = slope_ref[g * A_HPG + slot] * float(d)
        for r in range(d):
            k_prev = v_prev = None
            for j in range(S // (d * blk)):
                start = r + d * blk * j
                rows = pl.ds(start, blk, stride=d) if d > 1 else pl.ds(start, blk)
                q = q_ref[rows, :].astype(BF16)
                k_cur = k_ref[rows, :].astype(BF16)
                v_cur = v_ref[rows, :].astype(BF16)
                if j == 0:
                    k, v, valid, dist = k_cur, v_cur, valid1, dist1
                else:
                    k = jnp.concatenate([k_prev, k_cur], axis=0)
                    v = jnp.concatenate([v_prev, v_cur], axis=0)
                    valid, dist = valid2, dist2
                k_prev, v_prev = k_cur, v_cur
                s = _mm_nt(q, k) * (HEAD_A ** -0.5)
                s = jnp.where(valid, s - slope * dist, NEG_INF)
                m = jnp.max(s, axis=-1, keepdims=True)
                p = jnp.exp(s - m)
                l = jnp.sum(p, axis=-1, keepdims=True)
                o_s[rows, :] = _mm(p, v) / l
                l_s[rows, :] = jnp.broadcast_to(m + jnp.log(l), (blk, HEAD_A))
    l0, l1, l2 = scr[1][...], scr[3][...], scr[5][...]
    m = jnp.maximum(jnp.maximum(l0, l1), l2)
    e0, e1, e2 = jnp.exp(l0 - m), jnp.exp(l1 - m), jnp.exp(l2 - m)
    o_ref[...] = ((e0 * scr[0][...] + e1 * scr[2][...] + e2 * scr[4][...]) / (e0 + e1 + e2)).astype(o_ref.dtype)


def _attention(cols, Bn, S):
    T = cols.shape[0]
    slopes = jnp.asarray([2.0 ** (-ALIBI_MAX_BIAS * (h + 1.0) / A_HEADS) for h in range(A_HEADS)], F32)

    def col(off, g):
        return pl.BlockSpec((S, HEAD_A), lambda b, s: (b, off // HEAD_A + g * A_HPG + s))

    in_specs = [pl.BlockSpec(memory_space=pltpu.SMEM)]
    for g in range(len(A_DILATIONS)):
        in_specs += [col(P_AQ, g), col(P_AK, g), col(P_AV, g)]
    return pl.pallas_call(
        functools.partial(_attn_body, S=S),
        grid=(Bn, A_HPG),
        in_specs=in_specs,
        out_specs=pl.BlockSpec((S, HEAD_A), lambda b, s: (b, s)),
        out_shape=jax.ShapeDtypeStruct((T, A_OUT), BF16),
        scratch_shapes=[pltpu.VMEM((S, HEAD_A), F32)] * (2 * len(A_DILATIONS)),
        compiler_params=_cparams(("parallel", "parallel")),
        name="attention",
    )(slopes, *([cols] * (3 * len(A_DILATIONS))))


def _group_masks(ng):
    ri = lax.broadcasted_iota(jnp.int32, (ng, GROUP, GROUP), 1)
    ci = lax.broadcasted_iota(jnp.int32, (ng, GROUP, GROUP), 2)
    same = (ri ^ ci) < CHUNK
    return same, same & (ri >= ci), same & (ri > ci)


def _neumann_inverse_bd(a, same):
    C = CHUNK
    ng = a.shape[0]
    spread = lambda x: jnp.where(same, jnp.concatenate([x] * (GROUP // C), axis=1), 0.0)
    ri = lax.broadcasted_iota(jnp.int32, (ng, C, GROUP), 1)
    ci = lax.broadcasted_iota(jnp.int32, (ng, C, GROUP), 2)
    pw = a[:, 0:C]
    for i in range(1, GROUP // C):
        pw = pw + a[:, i * C:(i + 1) * C]
    tinv = jnp.where(ri == (ci & (C - 1)), 1.0, 0.0) + pw
    pw = _bmm(pw, a)
    for _ in range(4):
        both = _bmm(jnp.concatenate([pw, tinv], axis=1), spread(pw))
        pw = both[:, :C]
        tinv = tinv + both[:, C:]
    tinv = tinv + _bmm(tinv, spread(pw))
    return spread(tinv)


def _gdn_body(alog_ref, dtb_ref, q_ref, k_ref, v_ref, z_ref, sm_ref, cw_ref, nw_ref,
              o_ref, S_ref, bq, bk, bv, *, tS):
    C, H, D, G = CHUNK, B_HEADS, B_DIM, GROUP
    n_c = tS // C
    nb = H * n_c
    ng = nb * C // G
    s = pl.program_id(1)

    @pl.when(s == 0)
    def _():
        S_ref[...] = jnp.zeros_like(S_ref)
        for buf in (bq, bk, bv):
            buf[0:8, :] = jnp.zeros((8, B_WIDTH), F32)

    def conv_silu(raw_ref, buf, off):
        buf[8:8 + tS, :] = raw_ref[...]
        y = cw_ref[0:1, off:off + B_WIDTH] * buf[5:5 + tS, :]
        for j in range(1, B_CONV):
            y = y + cw_ref[j:j + 1, off:off + B_WIDTH] * buf[5 + j:5 + j + tS, :]
        buf[0:8, :] = buf[tS:tS + 8, :]
        return _silu(y)

    def heads(x):
        return jnp.stack([x[:, h * D:(h + 1) * D] for h in range(H)], axis=0).reshape(nb, C, D)

    q = heads(conv_silu(q_ref, bq, 0))
    k = heads(conv_silu(k_ref, bk, B_WIDTH))
    v = heads(conv_silu(v_ref, bv, 2 * B_WIDTH))
    q = q * lax.rsqrt(jnp.sum(q * q, axis=-1, keepdims=True) + L2_EPS) * (D ** -0.5)
    k = k * lax.rsqrt(jnp.sum(k * k, axis=-1, keepdims=True) + L2_EPS)

    sm = sm_ref[:, 0:128]
    lane = lax.broadcasted_iota(jnp.int32, sm.shape, 1)
    ld, beta = [], []
    for h in range(H):
        alpha = jnp.sum(jnp.where(lane == SM_ALPHA + h, sm, 0.0), axis=-1, keepdims=True)
        beta_logit = jnp.sum(jnp.where(lane == SM_BETA + h, sm, 0.0), axis=-1, keepdims=True)
        a_neg = -jnp.exp(jnp.full((1, 1), alog_ref[h], F32))
        ld.append(a_neg * _softplus(alpha + dtb_ref[h]))
        beta.append(jax.nn.sigmoid(beta_logit))
    ld = jnp.stack(ld, axis=0).reshape(ng, G, 1)
    beta = jnp.stack(beta, axis=0).reshape(ng, G, 1)

    same, tril, strict = _group_masks(ng)
    ri = lax.broadcasted_iota(jnp.int32, (ng, G, G), 1)
    ci = lax.broadcasted_iota(jnp.int32, (ng, G, G), 2)
    grp = lambda t: t.reshape(ng, G, t.shape[-1])
    blk = lambda t: t.reshape(nb, C, t.shape[-1])
    q, k, v = grp(q), grp(k), grp(v)
    LD = jnp.broadcast_to(ld, (ng, G, G))
    g_row = jnp.sum(jnp.where(same & (ri <= ci), LD, 0.0), axis=1, keepdims=True)
    ld_row = jnp.sum(jnp.where(ri == ci, LD, 0.0), axis=1, keepdims=True)
    g_col = jnp.sum(jnp.where(tril, jnp.broadcast_to(ld_row, (ng, G, G)), 0.0), axis=2, keepdims=True)
    g_last = blk(g_col)[:, C - 1:C, :]
    decay = jnp.exp(jnp.where(tril, g_col - g_row, NEG_INF))
    eg = jnp.exp(g_col)
    kb = k * beta
    vb = v * beta
    kq = _bmm_nt(jnp.concatenate([kb, q], axis=1), k)
    m = jnp.where(strict, kq[:, :G] * decay, 0.0)
    att = jnp.where(tril, kq[:, G:] * decay, 0.0)
    tinv = _neumann_inverse_bd(-m, same)
    wu = _bmm(tinv, jnp.concatenate([kb * eg, vb], axis=2))
    k_dec = blk(k) * jnp.exp(g_last - blk(g_col))
    gz = _bmm_tn(k_dec, blk(wu))
    aw = _bmm(att, wu)
    q_eff = blk(q * eg - aw[:, :, :D])
    o0 = blk(aw[:, :, D:])
    cd = jnp.exp(g_last)
    S = [S_ref[h] for h in range(H)]
    outs = [[None] * n_c for _ in range(H)]
    for c in range(n_c):
        for h in range(H):
            i = h * n_c + c
            outs[h][c] = _mm(q_eff[i], S[h]) + o0[i]
            S[h] = S[h] * cd[i] - _mm(gz[i, :, :D], S[h]) + gz[i, :, D:]
    nw = nw_ref[...]
    for h in range(H):
        S_ref[h] = S[h]
        o = jnp.concatenate(outs[h], axis=0)
        hs = slice(h * D, (h + 1) * D)
        o_ref[:, hs] = (_rms_rows(o, nw) * _silu(z_ref[:, hs])).astype(o_ref.dtype)


def _gdn(cols, b_conv, a_log, dt_bias, norm_w, Bn, S, tS=128):
    T = cols.shape[0]
    nS = S // tS
    W = B_WIDTH

    def col(off):
        return pl.BlockSpec((tS, W), lambda b, s: (b * nS + s, off // W))

    smem = pl.BlockSpec(memory_space=pltpu.SMEM)
    return pl.pallas_call(
        functools.partial(_gdn_body, tS=tS),
        grid=(Bn, nS),
        in_specs=[smem, smem, col(P_BQ), col(P_BK), col(P_BV), col(P_BZ),
                  pl.BlockSpec((tS, SM_W), lambda b, s: (b * nS + s, P_SM // SM_W)),
                  pl.BlockSpec((B_CONV, 3 * W), lambda b, s: (0, 0)),
                  pl.BlockSpec((1, B_DIM), lambda b, s: (0, 0))],
        out_specs=pl.BlockSpec((tS, W), lambda b, s: (b * nS + s, 0)),
        out_shape=jax.ShapeDtypeStruct((T, W), BF16),
        scratch_shapes=[pltpu.VMEM((B_HEADS, B_DIM, B_DIM), F32)] + [pltpu.VMEM((tS + 8, W), F32)] * 3,
        compiler_params=_cparams(("parallel", "arbitrary")),
        name="gdn",
    )(a_log, dt_bias, cols, cols, cols, cols, cols, b_conv, norm_w.reshape(1, B_DIM))


def _rwkv_body(r_ref, k_ref, v_ref, sm_ref, mur_ref, muk_ref, muv_ref, musm_ref,
               w0_ref, w2_ref, a0_ref, a2_ref, g2_ref, kkw_ref, kaw_ref, rk_ref, gnw_ref, gnb_ref,
               o_ref, S_ref, br, bk, bv, bsm, *, tS):
    C = CHUNK
    P = 2 * C_DIM
    NP = C_WIDTH // P
    s = pl.program_id(1)

    @pl.when(s == 0)
    def _():
        S_ref[...] = jnp.zeros_like(S_ref)
        for buf in (br, bk, bv, bsm):
            buf[0:8, :] = jnp.zeros((8, buf.shape[1]), F32)

    def shift(x_ref, buf, mu_ref):
        x = x_ref[...]
        buf[8:8 + tS, :] = x
        prev = buf[7:7 + tS, :]
        buf[0:8, :] = buf[tS:tS + 8, :]
        return x + (prev - x) * mu_ref[...]

    r = shift(r_ref, br, mur_ref)
    k = shift(k_ref, bk, muk_ref)
    v = shift(v_ref, bv, muv_ref)
    sm = shift(sm_ref, bsm, musm_ref)
    xw, xa, xg = sm[:, 0:128], sm[:, SM_XA:SM_XA + 128], sm[:, SM_XG:SM_XG + C_GATE_LORA]
    w = -_softplus(-(w0_ref[...] + _mm(jnp.tanh(xw), w2_ref[...]))) - 0.5
    a = jax.nn.sigmoid(a0_ref[...] + _mm(xa, a2_ref[...]))
    lw = -jnp.exp(w)
    gate = _mm(jax.nn.sigmoid(xg), g2_ref[...])
    kkraw = k * kkw_ref[...]
    k = k * (1.0 + (a - 1.0) * kaw_ref[...])

    n_c = tS // C
    nb = NP * n_c
    ng = nb * 2 * C // GROUP
    m0 = lax.broadcasted_iota(jnp.int32, (1, P), 1) < C_DIM
    rs = lax.broadcasted_iota(jnp.int32, (tS, tS), 0)
    cs = lax.broadcasted_iota(jnp.int32, (tS, tS), 1)
    tril_b = jnp.where((rs >= cs) & ((rs ^ cs) < C), 1.0, 0.0).astype(BF16)
    r2 = lax.broadcasted_iota(jnp.int32, (1, 2 * C, 2 * C), 1)
    c2 = lax.broadcasted_iota(jnp.int32, (1, 2 * C, 2 * C), 2)
    strict2 = r2 > c2
    rt = lax.broadcasted_iota(jnp.int32, (1, C, 2 * C), 1)
    ct = lax.broadcasted_iota(jnp.int32, (1, C, 2 * C), 2)
    incl = rt >= (ct & (C - 1))

    def seg_sum(x):
        s0 = jnp.sum(jnp.where(m0, x, 0.0), axis=-1, keepdims=True)
        s1 = jnp.sum(jnp.where(m0, 0.0, x), axis=-1, keepdims=True)
        return jnp.where(m0, s0, s1)

    def pairs(x):
        return jnp.stack([x[:, p * P:(p + 1) * P] for p in range(NP)], axis=0).reshape(nb, C, P)

    def stack(x):
        return jnp.concatenate([jnp.where(m0, x, 0.0), jnp.where(m0, 0.0, x)], axis=1)

    lw_hi = lw.astype(BF16)
    lw_lo = (lw - lw_hi.astype(F32)).astype(BF16)
    cum2 = jnp.dot(tril_b, jnp.concatenate([lw_hi, lw_lo], axis=1), preferred_element_type=F32)
    cum = cum2[:, :C_WIDTH] + cum2[:, C_WIDTH:]
    kkraw = pairs(kkraw)
    kk = kkraw * lax.rsqrt(seg_sum(kkraw * kkraw) + L2_EPS)
    bv = kk * pairs(a)
    w_inc = pairs(jnp.exp(cum))
    w_inv = pairs(jnp.exp(-cum))
    w_end = w_inc[:, C - 1:C, :]
    r_t = pairs(r) * w_inc
    a_st = stack(-kk * pairs(jnp.exp(cum - lw)))
    b_st = stack(bv * w_inv)
    k_st = stack(pairs(k) * w_inv)
    v_st = stack(pairs(v))
    aa = _bmm_nt(jnp.concatenate([a_st, r_t], axis=1), jnp.concatenate([b_st, k_st], axis=1))
    a_ab = jnp.where(strict2, aa[:, :2 * C, :2 * C], 0.0)
    a_ak = jnp.where(strict2, aa[:, :2 * C, 2 * C:], 0.0)
    a_rb = jnp.where(incl, aa[:, 2 * C:, :2 * C], 0.0)
    a_rk = jnp.where(incl, aa[:, 2 * C:, 2 * C:], 0.0)
    a4 = a_ab.reshape(ng, 2, 2 * C, 2 * C)
    zero = jnp.zeros((ng, 2 * C, 2 * C), F32)
    a_bd = jnp.concatenate([jnp.concatenate([a4[:, 0], zero], axis=2),
                            jnp.concatenate([zero, a4[:, 1]], axis=2)], axis=1)
    tinv = _neumann_inverse_bd(a_bd, _group_masks(ng)[0])
    x = jnp.concatenate([a_st, _bmm(a_ak, v_st)], axis=2)
    wu = _bmm(tinv, x.reshape(ng, GROUP, 2 * P)).reshape(nb, 2 * C, 2 * P)
    rw = _bmm(a_rb, wu)
    r_eff = r_t + rw[:, :, :P]
    o0 = rw[:, :, P:] + _bmm(a_rk, v_st)
    mz = _bmm_tn(wu, b_st * w_end)
    z = mz[:, P:, :] + _bmm_tn(v_st, k_st * w_end)
    S = [S_ref[p] for p in range(NP)]
    outs = [[None] * n_c for _ in range(NP)]
    for c in range(n_c):
        for p in range(NP):
            i = p * n_c + c
            outs[p][c] = _mm_nt(r_eff[i], S[p]) + o0[i]
            S[p] = S[p] * w_end[i] + _mm(S[p], mz[i, :P, :]) + z[i]
    for p in range(NP):
        S_ref[p] = S[p]
        ps = slice(p * P, (p + 1) * P)
        o = jnp.concatenate(outs[p], axis=0)
        mean = seg_sum(o) * (1.0 / C_DIM)
        d = o - mean
        var = seg_sum(d * d) * (1.0 / C_DIM)
        on = d * lax.rsqrt(var + C_GN_EPS) * gnw_ref[:, ps] + gnb_ref[:, ps]
        bonus = seg_sum(r[:, ps] * k[:, ps] * rk_ref[:, ps]) * v[:, ps]
        o_ref[:, ps] = ((on + bonus) * gate[:, ps]).astype(o_ref.dtype)


def _rwkv(cols, mu_r, mu_k, mu_v, mu_sm, w0, w2p, a0, a2p, g2, k_k, k_a, r_k, gn_w, gn_b, Bn, S, tS=128):
    T = cols.shape[0]
    nS = S // tS
    W = C_WIDTH
    P = 2 * C_DIM

    def col(off, width):
        return pl.BlockSpec((tS, width), lambda b, s: (b * nS + s, off // width))

    def par(rows, width=W):
        return pl.BlockSpec((rows, width), lambda b, s: (0, 0))

    return pl.pallas_call(
        functools.partial(_rwkv_body, tS=tS),
        grid=(Bn, nS),
        in_specs=[col(P_CR, W), col(P_CK, W), col(P_CV, W), col(P_SM, SM_W), par(1), par(1), par(1),
                  par(1, SM_W), par(1), par(128), par(1), par(128), par(C_GATE_LORA), par(1), par(1),
                  par(1), par(1), par(1)],
        out_specs=pl.BlockSpec((tS, W), lambda b, s: (b * nS + s, 0)),
        out_shape=jax.ShapeDtypeStruct((T, W), BF16),
        scratch_shapes=[pltpu.VMEM((W // P, P, P), F32)] + [pltpu.VMEM((tS + 8, W), F32)] * 3
                       + [pltpu.VMEM((tS + 8, SM_W), F32)],
        compiler_params=_cparams(("parallel", "arbitrary")),
        name="rwkv",
    )(cols, cols, cols, cols, mu_r, mu_k, mu_v, mu_sm, w0, w2p, a0, a2p, g2, k_k, k_a, r_k, gn_w, gn_b)


def _merge_body(x_ref, ya_ref, yb_ref, yc_ref, g_ref, pa_ref, pb_ref, pc_ref, wo_ref, o_ref):
    D = x_ref.shape[1]
    merged = (jax.nn.sigmoid(g_ref[:, 0:D]) * jnp.dot(ya_ref[...], pa_ref[...], preferred_element_type=F32)
              + jax.nn.sigmoid(g_ref[:, D:2 * D]) * jnp.dot(yb_ref[...], pb_ref[...], preferred_element_type=F32)
              + jax.nn.sigmoid(g_ref[:, 2 * D:]) * jnp.dot(yc_ref[...], pc_ref[...], preferred_element_type=F32))
    o_ref[...] = x_ref[...] + jnp.dot(merged.astype(BF16), wo_ref[...], preferred_element_type=F32)


def _merge(x, y_a, y_b, y_c, cols, proj_a, proj_b, proj_c, w_out, l, tm=256):
    T, D = x.shape

    def rows(width):
        return pl.BlockSpec((tm, width), lambda i: (i, 0))

    def resident(kdim):
        return pl.BlockSpec((None, kdim, D), lambda i: (l, 0, 0), pipeline_mode=pl.Buffered(1))

    return pl.pallas_call(
        _merge_body,
        grid=(T // tm,),
        in_specs=[rows(D), rows(A_OUT), rows(B_WIDTH), rows(C_WIDTH),
                  pl.BlockSpec((tm, 3 * D), lambda i: (i, P_G // (3 * D))),
                  resident(A_OUT), resident(B_WIDTH), resident(C_WIDTH), resident(D)],
        out_specs=rows(D),
        out_shape=jax.ShapeDtypeStruct((T, D), F32),
        compiler_params=_cparams(("parallel",)),
        name="merge",
    )(x, y_a, y_b, y_c, cols, proj_a, proj_b, proj_c, w_out)


def _cast_body(w_ref, o_ref):
    o_ref[...] = w_ref[...].astype(BF16)


def _to_bf16(w):
    L, R, Cw = w.shape
    tr = 128 if Cw > 4096 else 256
    spec = pl.BlockSpec((None, tr, Cw), lambda l, i: (l, i, 0))
    return pl.pallas_call(
        _cast_body,
        grid=(L, R // tr),
        in_specs=[spec],
        out_specs=spec,
        out_shape=jax.ShapeDtypeStruct(w.shape, BF16),
        compiler_params=_cparams(("parallel", "parallel")),
        name="cast_bf16",
    )(w)


def _pack_body(w_ref, o_ref):
    a_cols = 3 * A_HEADS * HEAD_A
    b_ab = a_cols + 4 * B_WIDTH
    c0 = b_ab + 2 * B_HEADS
    c_xw = c0 + 3 * C_WIDTH
    c_xa = c_xw + C_DECAY_LORA
    c_xg = c_xa + C_ICLR_LORA
    g0 = c_xg + C_GATE_LORA
    rows = w_ref.shape[0]

    def put(dst, src, n):
        o_ref[:, dst:dst + n] = w_ref[:, src:src + n].astype(BF16)

    put(P_G, g0, 3 * D_MODEL)
    put(P_BQ, a_cols, 4 * B_WIDTH)
    put(P_CR, c0, 3 * C_WIDTH)
    put(P_AQ, 0, a_cols)
    zeros = lambda n: jnp.zeros((rows, n), F32)
    small = jnp.concatenate([w_ref[:, c_xw:c_xa], w_ref[:, b_ab:c0], zeros(16),
                             w_ref[:, c_xa:c_xg], zeros(32), w_ref[:, c_xg:g0]], axis=1)
    o_ref[:, P_SM:P_SM + SM_W] = small.astype(BF16)


def _pack_w_in(w, tr=128):
    L, R, Cw = w.shape
    return pl.pallas_call(
        _pack_body,
        grid=(L, R // tr),
        in_specs=[pl.BlockSpec((None, tr, Cw), lambda l, i: (l, i, 0))],
        out_specs=pl.BlockSpec((None, tr, P_COLS), lambda l, i: (l, i, 0)),
        out_shape=jax.ShapeDtypeStruct((L, R, P_COLS), BF16),
        compiler_params=_cparams(("parallel", "parallel")),
        name="pack_w_in",
    )(w)


def _pad_rows(w, n):
    return jnp.concatenate([w, jnp.zeros((n - w.shape[0], w.shape[1]), w.dtype)], axis=0)


def _mixers(x, cols, l, Bn, S, b_conv, b_a_log, b_dt_bias, b_norm, c_mu, c_w0, c_w2, c_a0, c_a2, c_g2,
            c_k_k, c_k_a, c_r_k, c_gn_w, c_gn_b):
    y_a = _attention(cols, Bn, S)
    y_b = _gdn(cols, b_conv[l], b_a_log[l], b_dt_bias[l], b_norm[l], Bn, S)
    W = C_WIDTH
    mu = c_mu[l]
    z = lambda n: jnp.zeros((n,), F32)
    mu_sm = jnp.concatenate([mu[3 * W:3 * W + C_DECAY_LORA], z(32),
                             mu[3 * W + C_DECAY_LORA:3 * W + C_DECAY_LORA + C_ICLR_LORA], z(32),
                             mu[3 * W + C_DECAY_LORA + C_ICLR_LORA:]])
    row = lambda t: t.reshape(1, -1)
    y_c = _rwkv(cols, row(mu[0:W]), row(mu[W:2 * W]), row(mu[2 * W:3 * W]), row(mu_sm),
                row(c_w0[l]), _pad_rows(c_w2[l], 128).astype(BF16), row(c_a0[l]),
                _pad_rows(c_a2[l], 128).astype(BF16), c_g2[l].astype(BF16),
                row(c_k_k[l]), row(c_k_a[l]), row(c_r_k[l]), row(c_gn_w[l]), row(c_gn_b[l]), Bn, S)
    return y_a, y_b, y_c


def kernel(x, ffn1_norm, ffn1_w_gu, ffn1_w_down, mix_norm, w_in, b_conv, b_a_log, b_dt_bias, b_norm,
           c_mu, c_w0, c_w2, c_a0, c_a2, c_g2, c_k_k, c_k_a, c_r_k, c_gn_w, c_gn_b,
           proj_a, proj_b, proj_c, w_out, ffn2_norm, ffn2_w_gu, ffn2_w_down, final_norm):
    Bn, S, D = x.shape
    depth = w_in.shape[0]
    x = x.reshape(Bn * S, D)
    f1_gu, f1_dn, f2_gu, f2_dn = (_to_bf16(w) for w in (ffn1_w_gu, ffn1_w_down, ffn2_w_gu, ffn2_w_down))
    p_a, p_b, p_c, w_o = (_to_bf16(w) for w in (proj_a, proj_b, proj_c, w_out))
    w_packed = _pack_w_in(w_in)
    for l in range(depth):
        x = _ffn(x, ffn1_norm[l], f1_gu, f1_dn, l)
        cols = _inproj(x, mix_norm[l], w_packed, l)
        y_a, y_b, y_c = _mixers(x, cols, l, Bn, S, b_conv, b_a_log, b_dt_bias, b_norm, c_mu, c_w0,
                                c_w2, c_a0, c_a2, c_g2, c_k_k, c_k_a, c_r_k, c_gn_w, c_gn_b)
        x = _merge(x, y_a, y_b, y_c, cols, p_a, p_b, p_c, w_o, l)
        x = _ffn(x, ffn2_norm[l], f2_gu, f2_dn, l, out_norm=final_norm if l == depth - 1 else None)
    return x.reshape(Bn, S, D)
```

```python
import functools

import jax
import jax.numpy as jnp
from jax import lax
from jax.experimental import pallas as pl
from jax.experimental.pallas import tpu as pltpu

F32 = jnp.float32
BF16 = jnp.bfloat16

D_MODEL = 2048
D_FF = 5632
A_DILATIONS = (1, 4, 16)
A_HPG = 4
A_HEADS = 12
HEAD_A = 128
A_OUT = A_HPG * HEAD_A
A_STEPS = 128
ALIBI_MAX_BIAS = 8.0
B_HEADS = 8
B_DIM = 128
B_WIDTH = B_HEADS * B_DIM
B_CONV = 4
C_DIM = 64
C_HEADS = 16
C_WIDTH = C_HEADS * C_DIM
C_DECAY_LORA = 96
C_ICLR_LORA = 96
C_GATE_LORA = 256
C_GN_EPS = 64e-5
NORM_EPS = 1e-6
L2_EPS = 1e-6
NEG_INF = -1e30
LANES = 128
CHUNK = 64
GROUP = 4 * CHUNK

P_G = 0
P_BQ, P_BK, P_BV, P_BZ = 6144, 7168, 8192, 9216
P_CR, P_CK, P_CV = 10240, 11264, 12288
P_AQ, P_AK, P_AV = 13312, 14848, 16384
P_SM = 17920
P_COLS = 18432
SM_W = 512
SM_ALPHA, SM_BETA, SM_XA, SM_XG = 96, 104, 128, 256

VMEM_LIMIT = 56 * 1024 * 1024


def _cparams(sem):
    return pltpu.CompilerParams(dimension_semantics=sem, vmem_limit_bytes=VMEM_LIMIT)


def _mm(a, b):
    return jnp.dot(a.astype(BF16), b.astype(BF16), preferred_element_type=F32)


def _mm_nt(a, b):
    return lax.dot_general(a.astype(BF16), b.astype(BF16), (((1,), (1,)), ((), ())),
                           preferred_element_type=F32)


def _mm_tn(a, b):
    return lax.dot_general(a.astype(BF16), b.astype(BF16), (((0,), (0,)), ((), ())),
                           preferred_element_type=F32)


def _bmm(a, b):
    return jnp.einsum("bmk,bkn->bmn", a.astype(BF16), b.astype(BF16), preferred_element_type=F32)


def _bmm_nt(a, b):
    return jnp.einsum("bmk,bnk->bmn", a.astype(BF16), b.astype(BF16), preferred_element_type=F32)


def _bmm_tn(a, b):
    return jnp.einsum("bkm,bkn->bmn", a.astype(BF16), b.astype(BF16), preferred_element_type=F32)


def _silu(x):
    return x * jax.nn.sigmoid(x)


def _softplus(x):
    return jnp.maximum(x, 0.0) + jnp.log1p(jnp.exp(-jnp.abs(x)))


def _rms_rows(x, g):
    ms = jnp.mean(x * x, axis=-1, keepdims=True)
    return x * lax.rsqrt(ms + NORM_EPS) * g


def _ffn_body(x_ref, g_ref, wg_ref, wu_ref, wd_ref, *rest, with_out_norm):
    o_ref, h_ref = rest[-2], rest[-1]

    @pl.when(pl.program_id(1) == 0)
    def _():
        x = x_ref[...]
        h_ref[...] = _rms_rows(x, g_ref[...]).astype(BF16)
        o_ref[...] = x

    h = h_ref[...]
    gate = jnp.dot(h, wg_ref[...], preferred_element_type=F32)
    up = jnp.dot(h, wu_ref[...], preferred_element_type=F32)
    act = (0.5 * _silu(gate) * up).astype(BF16)
    o_ref[...] += jnp.dot(act, wd_ref[...], preferred_element_type=F32)

    if with_out_norm:
        @pl.when(pl.program_id(1) == pl.num_programs(1) - 1)
        def _():
            o_ref[...] = _rms_rows(o_ref[...], rest[0][...])


def _ffn(x, g, w_gu, w_down, l, out_norm=None, tm=1024, tf=512):
    T, D = x.shape
    dff = w_down.shape[1]
    nj = dff // tf
    vec = pl.BlockSpec((1, D), lambda i, j: (0, 0))
    extra = [] if out_norm is None else [out_norm.reshape(1, D)]
    return pl.pallas_call(
        functools.partial(_ffn_body, with_out_norm=out_norm is not None),
        grid=(T // tm, nj),
        in_specs=[
            pl.BlockSpec((tm, D), lambda i, j: (i, 0)),
            vec,
            pl.BlockSpec((None, D, tf), lambda i, j: (l, 0, j)),
            pl.BlockSpec((None, D, tf), lambda i, j: (l, 0, j + nj)),
            pl.BlockSpec((None, tf, D), lambda i, j: (l, j, 0)),
        ] + [vec] * len(extra),
        out_specs=pl.BlockSpec((tm, D), lambda i, j: (i, 0)),
        out_shape=jax.ShapeDtypeStruct((T, D), F32),
        scratch_shapes=[pltpu.VMEM((tm, D), BF16)],
        compiler_params=_cparams(("parallel", "arbitrary")),
        name="ffn",
    )(x, g.reshape(1, D), w_gu, w_gu, w_down, *extra)


def _inproj_body(x_ref, g_ref, w_ref, o_ref, h_ref):
    @pl.when(pl.program_id(1) == 0)
    def _():
        h_ref[...] = _rms_rows(x_ref[...], g_ref[...]).astype(BF16)

    o_ref[...] = _mm_nt(h_ref[...], w_ref[...])


def _inproj(x, g, w, l, tm=1024, tn=1024):
    T, D = x.shape
    N = w.shape[1]
    return pl.pallas_call(
        _inproj_body,
        grid=(T // tm, N // tn),
        in_specs=[
            pl.BlockSpec((tm, D), lambda i, j: (i, 0)),
            pl.BlockSpec((1, D), lambda i, j: (0, 0)),
            pl.BlockSpec((None, tn, D), lambda i, j: (l, j, 0)),
        ],
        out_specs=pl.BlockSpec((tm, tn), lambda i, j: (i, j)),
        out_shape=jax.ShapeDtypeStruct((T, N), F32),
        scratch_shapes=[pltpu.VMEM((tm, D), BF16)],
        compiler_params=_cparams(("parallel", "arbitrary")),
        name="inproj",
    )(x, g.reshape(1, D), w)


def _attn_body(slope_ref, *refs, S):
    n_g = len(A_DILATIONS)
    ins, o_ref, scr = refs[:3 * n_g], refs[3 * n_g], refs[3 * n_g + 1:]
    blk = A_STEPS
    slot = pl.program_id(1)
    qi = lax.broadcasted_iota(jnp.int32, (blk, 2 * blk), 0)
    ki = lax.broadcasted_iota(jnp.int32, (blk, 2 * blk), 1)
    delta2 = qi + blk - ki
    valid2 = (delta2 >= 0) & (delta2 <= A_STEPS)
    dist2 = delta2.astype(F32)
    delta1 = delta2[:, blk:]
    valid1 = delta1 >= 0
    dist1 = delta1.astype(F32)
    for g, d in enumerate(A_DILATIONS):
        q_ref, k_ref, v_ref = ins[3 * g:3 * g + 3]
        o_s, l_s = scr[2 * g], scr[2 * g + 1]
        slope = slope_ref[g * A_HPG + slot] * float(d)
        nj = S // (d * blk)
        rows = [pl.ds(r + d * blk * j, blk, stride=d) if d > 1 else pl.ds(blk * j, blk)
                for r in range(d) for j in range(nj)]
        nblk = len(rows)
        take = lambda ref: [ref[rw, :].astype(BF16) for rw in rows]
        q = jnp.stack(take(q_ref), axis=0)
        k_cur, v_cur = take(k_ref), take(v_ref)
        if nj > 1:
            none = jnp.zeros((blk, HEAD_A), BF16)
            prev = lambda t: [none if i % nj == 0 else t[i - 1] for i in range(nblk)]
            k = jnp.stack([jnp.concatenate(pc, axis=0) for pc in zip(prev(k_cur), k_cur)], axis=0)
            v = jnp.stack([jnp.concatenate(pc, axis=0) for pc in zip(prev(v_cur), v_cur)], axis=0)
            first = lax.broadcasted_iota(jnp.int32, (nblk, 1, 1), 0) % nj == 0
            valid = valid2 & ((ki >= blk) | jnp.logical_not(first))
            dist = dist2
        else:
            k, v, valid, dist = jnp.stack(k_cur, axis=0), jnp.stack(v_cur, axis=0), valid1, dist1
        s = _bmm_nt(q, k) * (HEAD_A ** -0.5)
        s = jnp.where(valid, s - slope * dist, NEG_INF)
        m = jnp.max(s, axis=-1, keepdims=True)
        p = jnp.exp(s - m)
        l = jnp.sum(p, axis=-1, keepdims=True)
        o = _bmm(p, v) / l
        lse = m + jnp.log(l)
        for i, rw in enumerate(rows):
            o_s[rw, :] = o[i]
            l_s[rw, :] = jnp.broadcast_to(lse[i], (blk, HEAD_A))
    l0, l1, l2 = scr[1][...], scr[3][...], scr[5][...]
    m = jnp.maximum(jnp.maximum(l0, l1), l2)
    e0, e1, e2 = jnp.exp(l0 - m), jnp.exp(l1 - m), jnp.exp(l2 - m)
    o_ref[...] = ((e0 * scr[0][...] + e1 * scr[2][...] + e2 * scr[4][...]) / (e0 + e1 + e2)).astype(o_ref.dtype)


def _attention(cols, Bn, S):
    T = cols.shape[0]
    slopes = jnp.asarray([2.0 ** (-ALIBI_MAX_BIAS * (h + 1.0) / A_HEADS) for h in range(A_HEADS)], F32)

    def col(off, g):
        return pl.BlockSpec((S, HEAD_A), lambda b, s: (b, off // HEAD_A + g * A_HPG + s))

    in_specs = [pl.BlockSpec(memory_space=pltpu.SMEM)]
    for g in range(len(A_DILATIONS)):
        in_specs += [col(P_AQ, g), col(P_AK, g), col(P_AV, g)]
    return pl.pallas_call(
        functools.partial(_attn_body, S=S),
        grid=(Bn, A_HPG),
        in_specs=in_specs,
        out_specs=pl.BlockSpec((S, HEAD_A), lambda b, s: (b, s)),
        out_shape=jax.ShapeDtypeStruct((T, A_OUT), BF16),
        scratch_shapes=[pltpu.VMEM((S, HEAD_A), F32)] * (2 * len(A_DILATIONS)),
        compiler_params=_cparams(("parallel", "parallel")),
        name="attention",
    )(slopes, *([cols] * (3 * len(A_DILATIONS))))


def _group_masks(ng):
    ri = lax.broadcasted_iota(jnp.int32, (ng, GROUP, GROUP), 1)
    ci = lax.broadcasted_iota(jnp.int32, (ng, GROUP, GROUP), 2)
    same = (ri ^ ci) < CHUNK
    return same, same & (ri >= ci), same & (ri > ci)


def _neumann_inverse_bd(a, same):
    C = CHUNK
    ng = a.shape[0]
    spread = lambda x: jnp.where(same, jnp.concatenate([x] * (GROUP // C), axis=1), 0.0)
    ri = lax.broadcasted_iota(jnp.int32, (ng, C, GROUP), 1)
    ci = lax.broadcasted_iota(jnp.int32, (ng, C, GROUP), 2)
    pw = a[:, 0:C]
    for i in range(1, GROUP // C):
        pw = pw + a[:, i * C:(i + 1) * C]
    tinv = jnp.where(ri == (ci & (C - 1)), 1.0, 0.0) + pw
    pw = _bmm(pw, a)
    for _ in range(4):
        both = _bmm(jnp.concatenate([pw, tinv], axis=1), spread(pw))
        pw = both[:, :C]
        tinv = tinv + both[:, C:]
    tinv = tinv + _bmm(tinv, spread(pw))
    return spread(tinv)


def _gdn_body(alog_ref, dtb_ref, q_ref, k_ref, v_ref, z_ref, sm_ref, cw_ref, nw_ref,
              o_ref, S_ref, bq, bk, bv, *, tS):
    C, H, D, G = CHUNK, B_HEADS, B_DIM, GROUP
    n_c = tS // C
    nb = H * n_c
    ng = nb * C // G
    s = pl.program_id(1)

    @pl.when(s == 0)
    def _():
        S_ref[...] = jnp.zeros_like(S_ref)
        for buf in (bq, bk, bv):
            buf[0:8, :] = jnp.zeros((8, B_WIDTH), F32)

    def conv_silu(raw_ref, buf, off):
        x = raw_ref[...]
        buf[8:8 + tS, :] = x
        full = buf[...]
        y = cw_ref[B_CONV - 1:B_CONV, off:off + B_WIDTH] * x
        for j in range(B_CONV - 1):
            y = y + cw_ref[j:j + 1, off:off + B_WIDTH] * pltpu.roll(full, B_CONV - 1 - j, axis=0)[8:8 + tS, :]
        buf[0:8, :] = buf[tS:tS + 8, :]
        return _silu(y)

    def heads(x):
        return jnp.stack([x[:, h * D:(h + 1) * D] for h in range(H)], axis=0).reshape(nb, C, D)

    q = heads(conv_silu(q_ref, bq, 0))
    k = heads(conv_silu(k_ref, bk, B_WIDTH))
    v = heads(conv_silu(v_ref, bv, 2 * B_WIDTH))
    q = q * lax.rsqrt(jnp.sum(q * q, axis=-1, keepdims=True) + L2_EPS) * (D ** -0.5)
    k = k * lax.rsqrt(jnp.sum(k * k, axis=-1, keepdims=True) + L2_EPS)

    sm = sm_ref[:, 0:LANES]
    lane = lax.broadcasted_iota(jnp.int32, sm.shape, 1)
    ld, beta = [], []
    for h in range(H):
        alpha = jnp.sum(jnp.where(lane == SM_ALPHA + h, sm, 0.0), axis=-1, keepdims=True)
        beta_logit = jnp.sum(jnp.where(lane == SM_BETA + h, sm, 0.0), axis=-1, keepdims=True)
        a_neg = -jnp.exp(jnp.full((1, 1), alog_ref[h], F32))
        ld.append(a_neg * _softplus(alpha + dtb_ref[h]))
        beta.append(jax.nn.sigmoid(beta_logit))
    ld = jnp.stack(ld, axis=0).reshape(ng, G, 1)
    beta = jnp.stack(beta, axis=0).reshape(ng, G, 1)

    same, tril, strict = _group_masks(ng)
    ri = lax.broadcasted_iota(jnp.int32, (ng, G, G), 1)
    ci = lax.broadcasted_iota(jnp.int32, (ng, G, G), 2)
    grp = lambda t: t.reshape(ng, G, t.shape[-1])
    blk = lambda t: t.reshape(nb, C, t.shape[-1])
    q, k, v = grp(q), grp(k), grp(v)
    LD = jnp.broadcast_to(ld, (ng, G, G))
    g_row = jnp.sum(jnp.where(same & (ri <= ci), LD, 0.0), axis=1, keepdims=True)
    ld_row = jnp.sum(jnp.where(ri == ci, LD, 0.0), axis=1, keepdims=True)
    g_col = jnp.sum(jnp.where(tril, jnp.broadcast_to(ld_row, (ng, G, G)), 0.0), axis=2, keepdims=True)
    g_last = blk(g_col)[:, C - 1:C, :]
    decay = jnp.exp(jnp.where(tril, g_col - g_row, NEG_INF))
    eg = jnp.exp(g_col)
    kb = k * beta
    vb = v * beta
    kq = _bmm_nt(jnp.concatenate([kb, q], axis=1), k)
    m = jnp.where(strict, kq[:, :G] * decay, 0.0)
    att = jnp.where(tril, kq[:, G:] * decay, 0.0)
    tinv = _neumann_inverse_bd(-m, same)
    wu = _bmm(tinv, jnp.concatenate([kb * eg, vb], axis=2))
    k_dec = blk(k) * jnp.exp(g_last - blk(g_col))
    gz = _bmm_tn(k_dec, blk(wu))
    aw = _bmm(att, wu)
    q_eff = blk(q * eg - aw[:, :, :D])
    o0 = blk(aw[:, :, D:])
    cd = jnp.exp(g_last)
    S = [S_ref[h] for h in range(H)]
    outs = [[None] * n_c for _ in range(H)]
    for c in range(n_c):
        for h in range(H):
            i = h * n_c + c
            outs[h][c] = _mm(q_eff[i], S[h]) + o0[i]
            S[h] = S[h] * cd[i] - _mm(gz[i, :, :D], S[h]) + gz[i, :, D:]
    nw = nw_ref[...]
    for h in range(H):
        S_ref[h] = S[h]
        o = jnp.concatenate(outs[h], axis=0)
        hs = slice(h * D, (h + 1) * D)
        o_ref[:, hs] = (_rms_rows(o, nw) * _silu(z_ref[:, hs])).astype(o_ref.dtype)


def _gdn(cols, b_conv, a_log, dt_bias, norm_w, Bn, S, tS=256):
    T = cols.shape[0]
    nS = S // tS
    W = B_WIDTH

    def col(off):
        return pl.BlockSpec((tS, W), lambda b, s: (b * nS + s, off // W))

    smem = pl.BlockSpec(memory_space=pltpu.SMEM)
    return pl.pallas_call(
        functools.partial(_gdn_body, tS=tS),
        grid=(Bn, nS),
        in_specs=[smem, smem, col(P_BQ), col(P_BK), col(P_BV), col(P_BZ),
                  pl.BlockSpec((tS, SM_W), lambda b, s: (b * nS + s, P_SM // SM_W)),
                  pl.BlockSpec((B_CONV, 3 * W), lambda b, s: (0, 0)),
                  pl.BlockSpec((1, B_DIM), lambda b, s: (0, 0))],
        out_specs=pl.BlockSpec((tS, W), lambda b, s: (b * nS + s, 0)),
        out_shape=jax.ShapeDtypeStruct((T, W), BF16),
        scratch_shapes=[pltpu.VMEM((B_HEADS, B_DIM, B_DIM), F32)] + [pltpu.VMEM((tS + 8, W), F32)] * 3,
        compiler_params=_cparams(("parallel", "arbitrary")),
        name="gdn",
    )(a_log, dt_bias, cols, cols, cols, cols, cols, b_conv, norm_w.reshape(1, B_DIM))


def _rwkv_body(r_ref, k_ref, v_ref, sm_ref, mur_ref, muk_ref, muv_ref, musm_ref,
               w0_ref, w2_ref, a0_ref, a2_ref, g2_ref, kkw_ref, kaw_ref, rk_ref, gnw_ref, gnb_ref,
               o_ref, S_ref, br, bk, bv, bsm, *, tS):
    C = CHUNK
    P = 2 * C_DIM
    NP = C_WIDTH // P
    s = pl.program_id(1)

    @pl.when(s == 0)
    def _():
        S_ref[...] = jnp.zeros_like(S_ref)
        for buf in (br, bk, bv, bsm):
            buf[0:8, :] = jnp.zeros((8, buf.shape[1]), F32)

    def shift(x_ref, buf, mu_ref):
        x = x_ref[...]
        buf[8:8 + tS, :] = x
        prev = pltpu.roll(buf[...], 1, axis=0)[8:8 + tS, :]
        buf[0:8, :] = buf[tS:tS + 8, :]
        return x + (prev - x) * mu_ref[...]

    r = shift(r_ref, br, mur_ref)
    k = shift(k_ref, bk, muk_ref)
    v = shift(v_ref, bv, muv_ref)
    sm = shift(sm_ref, bsm, musm_ref)
    xw, xa, xg = sm[:, 0:LANES], sm[:, SM_XA:SM_XA + LANES], sm[:, SM_XG:SM_XG + C_GATE_LORA]
    w = -_softplus(-(w0_ref[...] + _mm(jnp.tanh(xw), w2_ref[...]))) - 0.5
    a = jax.nn.sigmoid(a0_ref[...] + _mm(xa, a2_ref[...]))
    lw = -jnp.exp(w)
    gate = _mm(jax.nn.sigmoid(xg), g2_ref[...])
    kkraw = k * kkw_ref[...]
    k = k * (1.0 + (a - 1.0) * kaw_ref[...])

    n_c = tS // C
    nb = NP * n_c
    ng = nb * 2 * C // GROUP
    m0 = lax.broadcasted_iota(jnp.int32, (1, P), 1) < C_DIM
    rs = lax.broadcasted_iota(jnp.int32, (tS, tS), 0)
    cs = lax.broadcasted_iota(jnp.int32, (tS, tS), 1)
    tril_b = jnp.where((rs >= cs) & ((rs ^ cs) < C), 1.0, 0.0).astype(BF16)
    r2 = lax.broadcasted_iota(jnp.int32, (1, 2 * C, 2 * C), 1)
    c2 = lax.broadcasted_iota(jnp.int32, (1, 2 * C, 2 * C), 2)
    strict2 = r2 > c2
    rt = lax.broadcasted_iota(jnp.int32, (1, C, 2 * C), 1)
    ct = lax.broadcasted_iota(jnp.int32, (1, C, 2 * C), 2)
    incl = rt >= (ct & (C - 1))

    def seg_sum(x):
        s0 = jnp.sum(jnp.where(m0, x, 0.0), axis=-1, keepdims=True)
        s1 = jnp.sum(jnp.where(m0, 0.0, x), axis=-1, keepdims=True)
        return jnp.where(m0, s0, s1)

    def pairs(x):
        return jnp.stack([x[:, p * P:(p + 1) * P] for p in range(NP)], axis=0).reshape(nb, C, P)

    def stack(x):
        return jnp.concatenate([jnp.where(m0, x, 0.0), jnp.where(m0, 0.0, x)], axis=1)

    lw_hi = lw.astype(BF16)
    lw_lo = (lw - lw_hi.astype(F32)).astype(BF16)
    cum2 = jnp.dot(tril_b, jnp.concatenate([lw_hi, lw_lo], axis=1), preferred_element_type=F32)
    cum = cum2[:, :C_WIDTH] + cum2[:, C_WIDTH:]
    kkraw = pairs(kkraw)
    kk = kkraw * lax.rsqrt(seg_sum(kkraw * kkraw) + L2_EPS)
    bv = kk * pairs(a)
    w_inc = pairs(jnp.exp(cum))
    w_inv = pairs(jnp.exp(-cum))
    w_end = w_inc[:, C - 1:C, :]
    r_t = pairs(r) * w_inc
    a_st = stack(-kk * pairs(jnp.exp(cum - lw)))
    b_st = stack(bv * w_inv)
    k_st = stack(pairs(k) * w_inv)
    v_st = stack(pairs(v))
    aa = _bmm_nt(jnp.concatenate([a_st, r_t], axis=1), jnp.concatenate([b_st, k_st], axis=1))
    a_ab = jnp.where(strict2, aa[:, :2 * C, :2 * C], 0.0)
    a_ak = jnp.where(strict2, aa[:, :2 * C, 2 * C:], 0.0)
    a_rb = jnp.where(incl, aa[:, 2 * C:, :2 * C], 0.0)
    a_rk = jnp.where(incl, aa[:, 2 * C:, 2 * C:], 0.0)
    a4 = a_ab.reshape(ng, 2, 2 * C, 2 * C)
    zero = jnp.zeros((ng, 2 * C, 2 * C), F32)
    a_bd = jnp.concatenate([jnp.concatenate([a4[:, 0], zero], axis=2),
                            jnp.concatenate([zero, a4[:, 1]], axis=2)], axis=1)
    tinv = _neumann_inverse_bd(a_bd, _group_masks(ng)[0])
    x = jnp.concatenate([a_st, _bmm(a_ak, v_st)], axis=2)
    wu = _bmm(tinv, x.reshape(ng, GROUP, 2 * P)).reshape(nb, 2 * C, 2 * P)
    rw = _bmm(a_rb, wu)
    r_eff = r_t + rw[:, :, :P]
    o0 = rw[:, :, P:] + _bmm(a_rk, v_st)
    mz = _bmm_tn(wu, b_st * w_end)
    z = mz[:, P:, :] + _bmm_tn(v_st, k_st * w_end)
    S = [S_ref[p] for p in range(NP)]
    outs = [[None] * n_c for _ in range(NP)]
    for c in range(n_c):
        for p in range(NP):
            i = p * n_c + c
            outs[p][c] = _mm_nt(r_eff[i], S[p]) + o0[i]
            S[p] = S[p] * w_end[i] + _mm(S[p], mz[i, :P, :]) + z[i]
    for p in range(NP):
        S_ref[p] = S[p]
        ps = slice(p * P, (p + 1) * P)
        o = jnp.concatenate(outs[p], axis=0)
        mean = seg_sum(o) * (1.0 / C_DIM)
        d = o - mean
        var = seg_sum(d * d) * (1.0 / C_DIM)
        on = d * lax.rsqrt(var + C_GN_EPS) * gnw_ref[:, ps] + gnb_ref[:, ps]
        bonus = seg_sum(r[:, ps] * k[:, ps] * rk_ref[:, ps]) * v[:, ps]
        o_ref[:, ps] = ((on + bonus) * gate[:, ps]).astype(o_ref.dtype)


def _rwkv(cols, mu_r, mu_k, mu_v, mu_sm, w0, w2p, a0, a2p, g2, k_k, k_a, r_k, gn_w, gn_b, Bn, S, tS=256):
    T = cols.shape[0]
    nS = S // tS
    W = C_WIDTH
    P = 2 * C_DIM

    def col(off, width):
        return pl.BlockSpec((tS, width), lambda b, s: (b * nS + s, off // width))

    def par(rows, width=W):
        return pl.BlockSpec((rows, width), lambda b, s: (0, 0))

    return pl.pallas_call(
        functools.partial(_rwkv_body, tS=tS),
        grid=(Bn, nS),
        in_specs=[col(P_CR, W), col(P_CK, W), col(P_CV, W), col(P_SM, SM_W), par(1), par(1), par(1),
                  par(1, SM_W), par(1), par(LANES), par(1), par(LANES), par(C_GATE_LORA), par(1), par(1),
                  par(1), par(1), par(1)],
        out_specs=pl.BlockSpec((tS, W), lambda b, s: (b * nS + s, 0)),
        out_shape=jax.ShapeDtypeStruct((T, W), BF16),
        scratch_shapes=[pltpu.VMEM((W // P, P, P), F32)] + [pltpu.VMEM((tS + 8, W), F32)] * 3
                       + [pltpu.VMEM((tS + 8, SM_W), F32)],
        compiler_params=_cparams(("parallel", "arbitrary")),
        name="rwkv",
    )(cols, cols, cols, cols, mu_r, mu_k, mu_v, mu_sm, w0, w2p, a0, a2p, g2, k_k, k_a, r_k, gn_w, gn_b)


def _merge_body(x_ref, ya_ref, yb_ref, yc_ref, g_ref, pa_ref, pb_ref, pc_ref, wo_ref, o_ref):
    D = x_ref.shape[1]
    merged = (jax.nn.sigmoid(g_ref[:, 0:D]) * jnp.dot(ya_ref[...], pa_ref[...], preferred_element_type=F32)
              + jax.nn.sigmoid(g_ref[:, D:2 * D]) * jnp.dot(yb_ref[...], pb_ref[...], preferred_element_type=F32)
              + jax.nn.sigmoid(g_ref[:, 2 * D:]) * jnp.dot(yc_ref[...], pc_ref[...], preferred_element_type=F32))
    o_ref[...] = x_ref[...] + jnp.dot(merged.astype(BF16), wo_ref[...], preferred_element_type=F32)


def _merge(x, y_a, y_b, y_c, cols, proj_a, proj_b, proj_c, w_out, l, tm=256):
    T, D = x.shape

    def rows(width):
        return pl.BlockSpec((tm, width), lambda i: (i, 0))

    def resident(kdim):
        return pl.BlockSpec((None, kdim, D), lambda i: (l, 0, 0), pipeline_mode=pl.Buffered(1))

    return pl.pallas_call(
        _merge_body,
        grid=(T // tm,),
        in_specs=[rows(D), rows(A_OUT), rows(B_WIDTH), rows(C_WIDTH),
                  pl.BlockSpec((tm, 3 * D), lambda i: (i, P_G // (3 * D))),
                  resident(A_OUT), resident(B_WIDTH), resident(C_WIDTH), resident(D)],
        out_specs=rows(D),
        out_shape=jax.ShapeDtypeStruct((T, D), F32),
        compiler_params=_cparams(("parallel",)),
        name="merge",
    )(x, y_a, y_b, y_c, cols, proj_a, proj_b, proj_c, w_out)


def _cast_body(w_ref, o_ref):
    o_ref[...] = w_ref[...].astype(BF16)


def _to_bf16(w):
    L, R, Cw = w.shape
    tr = 128 if Cw > 4096 else 256
    spec = pl.BlockSpec((None, tr, Cw), lambda l, i: (l, i, 0))
    return pl.pallas_call(
        _cast_body,
        grid=(L, R // tr),
        in_specs=[spec],
        out_specs=spec,
        out_shape=jax.ShapeDtypeStruct(w.shape, BF16),
        compiler_params=_cparams(("parallel", "parallel")),
        name="cast_bf16",
    )(w)


def _pack_w_in(w):
    a_cols = 3 * A_HEADS * HEAD_A
    b_ab = a_cols + 4 * B_WIDTH
    c0 = b_ab + 2 * B_HEADS
    c_xw = c0 + 3 * C_WIDTH
    c_xa = c_xw + C_DECAY_LORA
    c_xg = c_xa + C_ICLR_LORA
    g0 = c_xg + C_GATE_LORA
    wt = jnp.swapaxes(w, 1, 2)
    z = lambda n: jnp.zeros((w.shape[0], n, w.shape[1]), w.dtype)
    return jnp.concatenate([
        wt[:, g0:], wt[:, a_cols:b_ab], wt[:, c0:c_xw], wt[:, 0:a_cols],
        wt[:, c_xw:c_xa], wt[:, b_ab:c0], z(16), wt[:, c_xa:c_xg], z(32), wt[:, c_xg:g0],
    ], axis=1).astype(BF16)


def _pad_rows(w, n):
    return jnp.concatenate([w, jnp.zeros((n - w.shape[0], w.shape[1]), w.dtype)], axis=0)


def _mixers(x, cols, l, Bn, S, b_conv, b_a_log, b_dt_bias, b_norm, c_mu, c_w0, c_w2, c_a0, c_a2, c_g2,
            c_k_k, c_k_a, c_r_k, c_gn_w, c_gn_b):
    y_a = _attention(cols, Bn, S)
    y_b = _gdn(cols, b_conv[l], b_a_log[l], b_dt_bias[l], b_norm[l], Bn, S)
    W = C_WIDTH
    mu = c_mu[l]
    z = lambda n: jnp.zeros((n,), F32)
    mu_sm = jnp.concatenate([mu[3 * W:3 * W + C_DECAY_LORA], z(32),
                             mu[3 * W + C_DECAY_LORA:3 * W + C_DECAY_LORA + C_ICLR_LORA], z(32),
                             mu[3 * W + C_DECAY_LORA + C_ICLR_LORA:]])
    row = lambda t: t.reshape(1, -1)
    y_c = _rwkv(cols, row(mu[0:W]), row(mu[W:2 * W]), row(mu[2 * W:3 * W]), row(mu_sm),
                row(c_w0[l]), _pad_rows(c_w2[l], LANES).astype(BF16), row(c_a0[l]),
                _pad_rows(c_a2[l], LANES).astype(BF16), c_g2[l].astype(BF16),
                row(c_k_k[l]), row(c_k_a[l]), row(c_r_k[l]), row(c_gn_w[l]), row(c_gn_b[l]), Bn, S)
    return y_a, y_b, y_c


def kernel(x, ffn1_norm, ffn1_w_gu, ffn1_w_down, mix_norm, w_in, b_conv, b_a_log, b_dt_bias, b_norm,
           c_mu, c_w0, c_w2, c_a0, c_a2, c_g2, c_k_k, c_k_a, c_r_k, c_gn_w, c_gn_b,
           proj_a, proj_b, proj_c, w_out, ffn2_norm, ffn2_w_gu, ffn2_w_down, final_norm):
    Bn, S, D = x.shape
    depth = w_in.shape[0]
    x = x.reshape(Bn * S, D)
    f1_gu, f1_dn, f2_gu, f2_dn = (_to_bf16(w) for w in (ffn1_w_gu, ffn1_w_down, ffn2_w_gu, ffn2_w_down))
    p_a, p_b, p_c, w_o = (_to_bf16(w) for w in (proj_a, proj_b, proj_c, w_out))
    w_packed = _pack_w_in(w_in)
    for l in range(depth):
        x = _ffn(x, ffn1_norm[l], f1_gu, f1_dn, l)
        cols = _inproj(x, mix_norm[l], w_packed, l)
        y_a, y_b, y_c = _mixers(x, cols, l, Bn, S, b_conv, b_a_log, b_dt_bias, b_norm, c_mu, c_w0,
                                c_w2, c_a0, c_a2, c_g2, c_k_k, c_k_a, c_r_k, c_gn_w, c_gn_b)
        x = _merge(x, y_a, y_b, y_c, cols, p_a, p_b, p_c, w_o, l)
        x = _ffn(x, ffn2_norm[l], f2_gu, f2_dn, l, out_norm=final_norm if l == depth - 1 else None)
    return x.reshape(Bn, S, D)
```

```python
import functools

import jax
import jax.numpy as jnp
from jax import lax
from jax.experimental import pallas as pl
from jax.experimental.pallas import tpu as pltpu

F32 = jnp.float32
BF16 = jnp.bfloat16

D_MODEL = 2048
D_FF = 5632
A_DILATIONS = (1, 4, 16)
A_HPG = 4
A_HEADS = 12
HEAD_A = 128
A_OUT = A_HPG * HEAD_A
A_STEPS = 128
ALIBI_MAX_BIAS = 8.0
B_HEADS = 8
B_DIM = 128
B_WIDTH = B_HEADS * B_DIM
B_CONV = 4
C_DIM = 64
C_HEADS = 16
C_WIDTH = C_HEADS * C_DIM
C_DECAY_LORA = 96
C_ICLR_LORA = 96
C_GATE_LORA = 256
C_GN_EPS = 64e-5
C_DECAY_SCALE = 0.6065306597126334
NORM_EPS = 1e-6
L2_EPS = 1e-6
NEG_INF = -1e30
LANES = 128
CHUNK = 64
GROUP = 4 * CHUNK

P_G = 0
P_BQ, P_BK, P_BV, P_BZ = 6144, 7168, 8192, 9216
P_CR, P_CK, P_CV = 10240, 11264, 12288
P_AQ, P_AK, P_AV = 13312, 14848, 16384
P_SM = 17920
P_COLS = 18432
SM_W = 512
SM_ALPHA, SM_BETA, SM_XA, SM_XG = 96, 104, 128, 256

VMEM_LIMIT = 56 * 1024 * 1024


def _cparams(sem):
    return pltpu.CompilerParams(dimension_semantics=sem, vmem_limit_bytes=VMEM_LIMIT)


def _mm(a, b):
    return jnp.dot(a.astype(BF16), b.astype(BF16), preferred_element_type=F32)


def _mm_nt(a, b):
    return lax.dot_general(a.astype(BF16), b.astype(BF16), (((1,), (1,)), ((), ())),
                           preferred_element_type=F32)


def _mm_tn(a, b):
    return lax.dot_general(a.astype(BF16), b.astype(BF16), (((0,), (0,)), ((), ())),
                           preferred_element_type=F32)


def _bmm(a, b):
    return jnp.einsum("bmk,bkn->bmn", a.astype(BF16), b.astype(BF16), preferred_element_type=F32)


def _bmm_nt(a, b):
    return jnp.einsum("bmk,bnk->bmn", a.astype(BF16), b.astype(BF16), preferred_element_type=F32)


def _bmm_tn(a, b):
    return jnp.einsum("bkm,bkn->bmn", a.astype(BF16), b.astype(BF16), preferred_element_type=F32)


def _silu(x):
    return x * jax.nn.sigmoid(x)


def _softplus(x):
    return jnp.maximum(x, 0.0) + jnp.log1p(jnp.exp(-jnp.abs(x)))


def _rms_rows(x, g):
    ms = jnp.mean(x * x, axis=-1, keepdims=True)
    return x * lax.rsqrt(ms + NORM_EPS) * g


def _ffn_body(x_ref, g_ref, wg_ref, wu_ref, wd_ref, *rest, with_out_norm):
    o_ref, h_ref = rest[-2], rest[-1]

    @pl.when(pl.program_id(1) == 0)
    def _():
        x = x_ref[...]
        h_ref[...] = _rms_rows(x, g_ref[...]).astype(BF16)
        o_ref[...] = x

    h = h_ref[...]
    gate = jnp.dot(h, wg_ref[...], preferred_element_type=F32)
    up = jnp.dot(h, wu_ref[...], preferred_element_type=F32)
    act = (0.5 * _silu(gate) * up).astype(BF16)
    o_ref[...] += jnp.dot(act, wd_ref[...], preferred_element_type=F32)

    if with_out_norm:
        @pl.when(pl.program_id(1) == pl.num_programs(1) - 1)
        def _():
            o_ref[...] = _rms_rows(o_ref[...], rest[0][...])


def _ffn(x, g, w_gu, w_down, l, out_norm=None, tm=1024, tf=512):
    T, D = x.shape
    dff = w_down.shape[1]
    nj = dff // tf
    vec = pl.BlockSpec((1, D), lambda i, j: (0, 0))
    extra = [] if out_norm is None else [out_norm.reshape(1, D)]
    return pl.pallas_call(
        functools.partial(_ffn_body, with_out_norm=out_norm is not None),
        grid=(T // tm, nj),
        in_specs=[
            pl.BlockSpec((tm, D), lambda i, j: (i, 0)),
            vec,
            pl.BlockSpec((None, D, tf), lambda i, j: (l, 0, j)),
            pl.BlockSpec((None, D, tf), lambda i, j: (l, 0, j + nj)),
            pl.BlockSpec((None, tf, D), lambda i, j: (l, j, 0)),
        ] + [vec] * len(extra),
        out_specs=pl.BlockSpec((tm, D), lambda i, j: (i, 0)),
        out_shape=jax.ShapeDtypeStruct((T, D), F32),
        scratch_shapes=[pltpu.VMEM((tm, D), BF16)],
        compiler_params=_cparams(("parallel", "arbitrary")),
        name="ffn",
    )(x, g.reshape(1, D), w_gu, w_gu, w_down, *extra)


def _inproj_body(x_ref, g_ref, w_ref, o_ref, h_ref):
    @pl.when(pl.program_id(1) == 0)
    def _():
        h_ref[...] = _rms_rows(x_ref[...], g_ref[...]).astype(BF16)

    o_ref[...] = _mm_nt(h_ref[...], w_ref[...])


def _inproj(x, g, w, l, tm=1024, tn=1024):
    T, D = x.shape
    N = w.shape[1]
    return pl.pallas_call(
        _inproj_body,
        grid=(T // tm, N // tn),
        in_specs=[
            pl.BlockSpec((tm, D), lambda i, j: (i, 0)),
            pl.BlockSpec((1, D), lambda i, j: (0, 0)),
            pl.BlockSpec((None, tn, D), lambda i, j: (l, j, 0)),
        ],
        out_specs=pl.BlockSpec((tm, tn), lambda i, j: (i, j)),
        out_shape=jax.ShapeDtypeStruct((T, N), F32),
        scratch_shapes=[pltpu.VMEM((tm, D), BF16)],
        compiler_params=_cparams(("parallel", "arbitrary")),
        name="inproj",
    )(x, g.reshape(1, D), w)


def _attn_body(slope_ref, *refs, S):
    n_g = len(A_DILATIONS)
    ins, o_ref, scr = refs[:3 * n_g], refs[3 * n_g], refs[3 * n_g + 1:]
    blk = A_STEPS
    slot = pl.program_id(1)
    qi = lax.broadcasted_iota(jnp.int32, (blk, 2 * blk), 0)
    ki = lax.broadcasted_iota(jnp.int32, (blk, 2 * blk), 1)
    delta2 = qi + blk - ki
    valid2 = (delta2 >= 0) & (delta2 <= A_STEPS)
    dist2 = delta2.astype(F32)
    delta1 = delta2[:, blk:]
    valid1 = delta1 >= 0
    dist1 = delta1.astype(F32)
    for g, d in enumerate(A_DILATIONS):
        q_ref, k_ref, v_ref = ins[3 * g:3 * g + 3]
        o_s, l_s = scr[2 * g], scr[2 * g + 1]
        slope = slope_ref[g * A_HPG + slot] * float(d)
        nj = S // (d * blk)
        rows = [pl.ds(r + d * blk * j, blk, stride=d) if d > 1 else pl.ds(blk * j, blk)
                for r in range(d) for j in range(nj)]
        nblk = len(rows)
        take = lambda ref: [ref[rw, :].astype(BF16) for rw in rows]
        q = jnp.stack(take(q_ref), axis=0)
        k_cur, v_cur = take(k_ref), take(v_ref)
        if nj > 1:
            none = jnp.zeros((blk, HEAD_A), BF16)
            prev = lambda t: [none if i % nj == 0 else t[i - 1] for i in range(nblk)]
            k = jnp.stack([jnp.concatenate(pc, axis=0) for pc in zip(prev(k_cur), k_cur)], axis=0)
            v = jnp.stack([jnp.concatenate(pc, axis=0) for pc in zip(prev(v_cur), v_cur)], axis=0)
            first = lax.broadcasted_iota(jnp.int32, (nblk, 1, 1), 0) % nj == 0
            valid = valid2 & ((ki >= blk) | jnp.logical_not(first))
            dist = dist2
        else:
            k, v, valid, dist = jnp.stack(k_cur, axis=0), jnp.stack(v_cur, axis=0), valid1, dist1
        s = _bmm_nt(q, k) * (HEAD_A ** -0.5)
        s = jnp.where(valid, s - slope * dist, NEG_INF)
        m = jnp.max(s, axis=-1, keepdims=True)
        p = jnp.exp(s - m)
        l = jnp.sum(p, axis=-1, keepdims=True)
        o = _bmm(p, v) / l
        lse = m + jnp.log(l)
        for i, rw in enumerate(rows):
            o_s[rw, :] = o[i]
            l_s[rw, :] = jnp.broadcast_to(lse[i], (blk, HEAD_A))
    l0, l1, l2 = scr[1][...], scr[3][...], scr[5][...]
    m = jnp.maximum(jnp.maximum(l0, l1), l2)
    e0, e1, e2 = jnp.exp(l0 - m), jnp.exp(l1 - m), jnp.exp(l2 - m)
    o_ref[...] = ((e0 * scr[0][...] + e1 * scr[2][...] + e2 * scr[4][...]) / (e0 + e1 + e2)).astype(o_ref.dtype)


def _attention(cols, Bn, S):
    T = cols.shape[0]
    slopes = jnp.asarray([2.0 ** (-ALIBI_MAX_BIAS * (h + 1.0) / A_HEADS) for h in range(A_HEADS)], F32)

    def col(off, g):
        return pl.BlockSpec((S, HEAD_A), lambda b, s: (b, off // HEAD_A + g * A_HPG + s))

    in_specs = [pl.BlockSpec(memory_space=pltpu.SMEM)]
    for g in range(len(A_DILATIONS)):
        in_specs += [col(P_AQ, g), col(P_AK, g), col(P_AV, g)]
    return pl.pallas_call(
        functools.partial(_attn_body, S=S),
        grid=(Bn, A_HPG),
        in_specs=in_specs,
        out_specs=pl.BlockSpec((S, HEAD_A), lambda b, s: (b, s)),
        out_shape=jax.ShapeDtypeStruct((T, A_OUT), BF16),
        scratch_shapes=[pltpu.VMEM((S, HEAD_A), F32)] * (2 * len(A_DILATIONS)),
        compiler_params=_cparams(("parallel", "parallel")),
        name="attention",
    )(slopes, *([cols] * (3 * len(A_DILATIONS))))


def _group_masks(ng):
    ri = lax.broadcasted_iota(jnp.int32, (ng, GROUP, GROUP), 1)
    ci = lax.broadcasted_iota(jnp.int32, (ng, GROUP, GROUP), 2)
    same = (ri ^ ci) < CHUNK
    return same, same & (ri >= ci), same & (ri > ci)


def _neumann_inverse_bd(pw, same):
    C = CHUNK
    ng = pw.shape[0]
    spread = lambda x: jnp.where(same, jnp.concatenate([x] * (GROUP // C), axis=1), 0.0)
    ri = lax.broadcasted_iota(jnp.int32, (ng, C, GROUP), 1)
    ci = lax.broadcasted_iota(jnp.int32, (ng, C, GROUP), 2)
    tinv = jnp.where(ri == (ci & (C - 1)), 1.0, 0.0) + pw
    pw = _bmm(pw, spread(pw))
    for _ in range(4):
        both = _bmm(jnp.concatenate([pw, tinv], axis=1), spread(pw))
        pw = both[:, :C]
        tinv = tinv + both[:, C:]
    tinv = tinv + _bmm(tinv, spread(pw))
    return spread(tinv)


def _gdn_body(alog_ref, dtb_ref, q_ref, k_ref, v_ref, z_ref, sm_ref, cw_ref, nw_ref,
              o_ref, S_ref, bq, bk, bv, *, tS):
    C, H, D, G = CHUNK, B_HEADS, B_DIM, GROUP
    n_c = tS // C
    nb = H * n_c
    ng = nb * C // G
    s = pl.program_id(1)

    @pl.when(s == 0)
    def _():
        S_ref[...] = jnp.zeros_like(S_ref)
        for buf in (bq, bk, bv):
            buf[0:8, :] = jnp.zeros((8, B_WIDTH), F32)

    def conv_silu(raw_ref, buf, off):
        x = raw_ref[...]
        buf[8:8 + tS, :] = x
        full = buf[...]
        y = cw_ref[B_CONV - 1:B_CONV, off:off + B_WIDTH] * x
        for j in range(B_CONV - 1):
            y = y + cw_ref[j:j + 1, off:off + B_WIDTH] * pltpu.roll(full, B_CONV - 1 - j, axis=0)[8:8 + tS, :]
        buf[0:8, :] = buf[tS:tS + 8, :]
        return _silu(y)

    def heads(x):
        return jnp.stack([x[:, h * D:(h + 1) * D] for h in range(H)], axis=0).reshape(nb, C, D)

    q = heads(conv_silu(q_ref, bq, 0))
    k = heads(conv_silu(k_ref, bk, B_WIDTH))
    v = heads(conv_silu(v_ref, bv, 2 * B_WIDTH))
    q = q * (lax.rsqrt(jnp.sum(q * q, axis=-1, keepdims=True) + L2_EPS) * (D ** -0.5))
    k = k * lax.rsqrt(jnp.sum(k * k, axis=-1, keepdims=True) + L2_EPS)

    sm = sm_ref[:, 0:LANES]
    lane = lax.broadcasted_iota(jnp.int32, sm.shape, 1)
    lane1 = lax.broadcasted_iota(jnp.int32, (1, LANES), 1)
    dtb = jnp.zeros((1, LANES), F32)
    alog = jnp.zeros((1, LANES), F32)
    for h in range(H):
        dtb = jnp.where(lane1 == SM_ALPHA + h, dtb_ref[h], dtb)
        alog = jnp.where(lane1 == SM_ALPHA + h, alog_ref[h], alog)
    ld_all = -jnp.exp(alog) * _softplus(sm + dtb)
    beta_all = jax.nn.sigmoid(sm)
    ld, beta = [], []
    for h in range(H):
        ld.append(jnp.sum(jnp.where(lane == SM_ALPHA + h, ld_all, 0.0), axis=-1, keepdims=True))
        beta.append(jnp.sum(jnp.where(lane == SM_BETA + h, beta_all, 0.0), axis=-1, keepdims=True))
    ld = jnp.stack(ld, axis=0).reshape(ng, G, 1)
    beta = jnp.stack(beta, axis=0).reshape(ng, G, 1)

    same, tril, strict = _group_masks(ng)
    ri = lax.broadcasted_iota(jnp.int32, (ng, G, G), 1)
    ci = lax.broadcasted_iota(jnp.int32, (ng, G, G), 2)
    grp = lambda t: t.reshape(ng, G, t.shape[-1])
    blk = lambda t: t.reshape(nb, C, t.shape[-1])
    q, k, v = grp(q), grp(k), grp(v)
    LD = jnp.broadcast_to(ld, (ng, G, G))
    g_row = jnp.sum(jnp.where(same & (ri <= ci), LD, 0.0), axis=1, keepdims=True)
    ld_row = jnp.sum(jnp.where(ri == ci, LD, 0.0), axis=1, keepdims=True)
    g_col = jnp.sum(jnp.where(tril, jnp.broadcast_to(ld_row, (ng, G, G)), 0.0), axis=2, keepdims=True)
    g_last = blk(g_col)[:, C - 1:C, :]
    decay = jnp.exp(jnp.where(tril, g_col - g_row, NEG_INF))
    eg = jnp.exp(g_col)
    kb = k * beta
    vb = v * beta
    kq = _bmm_nt(jnp.concatenate([kb, q], axis=1), k)
    m = jnp.where(strict, kq[:, :G] * decay, 0.0)
    att = jnp.where(tril, kq[:, G:] * decay, 0.0)
    m_cat = m[:, 0:C]
    for i in range(1, G // C):
        m_cat = m_cat + m[:, i * C:(i + 1) * C]
    tinv = _neumann_inverse_bd(-m_cat, same)
    wu = _bmm(tinv, jnp.concatenate([kb * eg, vb], axis=2))
    k_dec = blk(k) * jnp.exp(g_last - blk(g_col))
    gz = _bmm_tn(k_dec, blk(wu))
    aw = _bmm(att, wu)
    q_eff = blk(q * eg - aw[:, :, :D])
    o0 = blk(aw[:, :, D:])
    cd = jnp.exp(g_last)
    S = [S_ref[h] for h in range(H)]
    outs = [[None] * n_c for _ in range(H)]
    for c in range(n_c):
        for h in range(H):
            i = h * n_c + c
            outs[h][c] = _mm(q_eff[i], S[h]) + o0[i]
            S[h] = S[h] * cd[i] - _mm(gz[i, :, :D], S[h]) + gz[i, :, D:]
    nw = nw_ref[...]
    for h in range(H):
        S_ref[h] = S[h]
        o = jnp.concatenate(outs[h], axis=0)
        hs = slice(h * D, (h + 1) * D)
        o_ref[:, hs] = (_rms_rows(o, nw) * _silu(z_ref[:, hs])).astype(o_ref.dtype)


def _gdn(cols, b_conv, a_log, dt_bias, norm_w, Bn, S, tS=256):
    T = cols.shape[0]
    nS = S // tS
    W = B_WIDTH

    def col(off):
        return pl.BlockSpec((tS, W), lambda b, s: (b * nS + s, off // W))

    smem = pl.BlockSpec(memory_space=pltpu.SMEM)
    return pl.pallas_call(
        functools.partial(_gdn_body, tS=tS),
        grid=(Bn, nS),
        in_specs=[smem, smem, col(P_BQ), col(P_BK), col(P_BV), col(P_BZ),
                  pl.BlockSpec((tS, SM_W), lambda b, s: (b * nS + s, P_SM // SM_W)),
                  pl.BlockSpec((B_CONV, 3 * W), lambda b, s: (0, 0)),
                  pl.BlockSpec((1, B_DIM), lambda b, s: (0, 0))],
        out_specs=pl.BlockSpec((tS, W), lambda b, s: (b * nS + s, 0)),
        out_shape=jax.ShapeDtypeStruct((T, W), BF16),
        scratch_shapes=[pltpu.VMEM((B_HEADS, B_DIM, B_DIM), F32)] + [pltpu.VMEM((tS + 8, W), F32)] * 3,
        compiler_params=_cparams(("parallel", "arbitrary")),
        name="gdn",
    )(a_log, dt_bias, cols, cols, cols, cols, cols, b_conv, norm_w.reshape(1, B_DIM))


def _rwkv_body(r_ref, k_ref, v_ref, sm_ref, mur_ref, muk_ref, muv_ref, musm_ref,
               w0_ref, w2_ref, a0_ref, a2_ref, g2_ref, kkw_ref, kaw_ref, rk_ref, gnw_ref, gnb_ref,
               o_ref, S_ref, br, bk, bv, bsm, *, tS):
    C = CHUNK
    P = 2 * C_DIM
    NP = C_WIDTH // P
    s = pl.program_id(1)

    @pl.when(s == 0)
    def _():
        S_ref[...] = jnp.zeros_like(S_ref)
        for buf in (br, bk, bv, bsm):
            buf[0:8, :] = jnp.zeros((8, buf.shape[1]), F32)

    def shift(x_ref, buf, mu_ref):
        x = x_ref[...]
        buf[8:8 + tS, :] = x
        prev = pltpu.roll(buf[...], 1, axis=0)[8:8 + tS, :]
        buf[0:8, :] = buf[tS:tS + 8, :]
        return x + (prev - x) * mu_ref[...]

    r = shift(r_ref, br, mur_ref)
    k = shift(k_ref, bk, muk_ref)
    v = shift(v_ref, bv, muv_ref)
    sm = shift(sm_ref, bsm, musm_ref)
    xw, xa, xg = sm[:, 0:LANES], sm[:, SM_XA:SM_XA + LANES], sm[:, SM_XG:SM_XG + C_GATE_LORA]
    lw = -C_DECAY_SCALE * jax.nn.sigmoid(w0_ref[...] + _mm(jnp.tanh(xw), w2_ref[...]))
    a = jax.nn.sigmoid(a0_ref[...] + _mm(xa, a2_ref[...]))
    gate = _mm(jax.nn.sigmoid(xg), g2_ref[...])
    kkraw = k * kkw_ref[...]
    k = k * (1.0 + (a - 1.0) * kaw_ref[...])

    n_c = tS // C
    nb = NP * n_c
    ng = nb * 2 * C // GROUP
    m0 = lax.broadcasted_iota(jnp.int32, (1, P), 1) < C_DIM
    rs = lax.broadcasted_iota(jnp.int32, (tS, tS), 0)
    cs = lax.broadcasted_iota(jnp.int32, (tS, tS), 1)
    tril_b = jnp.where((rs >= cs) & ((rs ^ cs) < C), 1.0, 0.0).astype(BF16)
    r2 = lax.broadcasted_iota(jnp.int32, (1, 2 * C, 2 * C), 1)
    c2 = lax.broadcasted_iota(jnp.int32, (1, 2 * C, 2 * C), 2)
    strict2 = r2 > c2
    rt = lax.broadcasted_iota(jnp.int32, (1, C, 2 * C), 1)
    ct = lax.broadcasted_iota(jnp.int32, (1, C, 2 * C), 2)
    incl = rt >= (ct & (C - 1))

    def seg_sum(x):
        s0 = jnp.sum(jnp.where(m0, x, 0.0), axis=-1, keepdims=True)
        s1 = jnp.sum(jnp.where(m0, 0.0, x), axis=-1, keepdims=True)
        return jnp.where(m0, s0, s1)

    def pairs(x):
        return jnp.stack([x[:, p * P:(p + 1) * P] for p in range(NP)], axis=0).reshape(nb, C, P)

    def stack(x):
        return jnp.concatenate([jnp.where(m0, x, 0.0), jnp.where(m0, 0.0, x)], axis=1)

    lw_hi = lw.astype(BF16)
    lw_lo = (lw - lw_hi.astype(F32)).astype(BF16)
    cum2 = jnp.dot(tril_b, jnp.concatenate([lw_hi, lw_lo], axis=1), preferred_element_type=F32)
    cum = cum2[:, :C_WIDTH] + cum2[:, C_WIDTH:]
    kkraw = pairs(kkraw)
    kk = kkraw * lax.rsqrt(seg_sum(kkraw * kkraw) + L2_EPS)
    bv = kk * pairs(a)
    w_inc = pairs(jnp.exp(cum))
    w_inv = pairs(jnp.exp(-cum))
    w_end = w_inc[:, C - 1:C, :]
    r_t = pairs(r) * w_inc
    a_st = stack(-kk * pairs(jnp.exp(cum - lw)))
    b_st = stack(bv * w_inv)
    k_st = stack(pairs(k) * w_inv)
    v_st = stack(pairs(v))
    aa = _bmm_nt(jnp.concatenate([a_st, r_t], axis=1), jnp.concatenate([b_st, k_st], axis=1))
    a_ab = jnp.where(strict2, aa[:, :2 * C, :2 * C], 0.0)
    a_ak = jnp.where(strict2, aa[:, :2 * C, 2 * C:], 0.0)
    a_rb = jnp.where(incl, aa[:, 2 * C:, :2 * C], 0.0)
    a_rk = jnp.where(incl, aa[:, 2 * C:, 2 * C:], 0.0)
    a4 = a_ab.reshape(ng, 2, 2 * C, 2 * C)
    a_cat = jnp.concatenate([a4[:, 0, :C] + a4[:, 0, C:], a4[:, 1, :C] + a4[:, 1, C:]], axis=2)
    tinv = _neumann_inverse_bd(a_cat, _group_masks(ng)[0])
    x = jnp.concatenate([a_st, _bmm(a_ak, v_st)], axis=2)
    wu = _bmm(tinv, x.reshape(ng, GROUP, 2 * P)).reshape(nb, 2 * C, 2 * P)
    rw = _bmm(a_rb, wu)
    r_eff = r_t + rw[:, :, :P]
    o0 = rw[:, :, P:] + _bmm(a_rk, v_st)
    mz = _bmm_tn(wu, b_st * w_end)
    z = mz[:, P:, :] + _bmm_tn(v_st, k_st * w_end)
    S = [S_ref[p] for p in range(NP)]
    outs = [[None] * n_c for _ in range(NP)]
    for c in range(n_c):
        for p in range(NP):
            i = p * n_c + c
            outs[p][c] = _mm_nt(r_eff[i], S[p]) + o0[i]
            S[p] = S[p] * w_end[i] + _mm(S[p], mz[i, :P, :]) + z[i]
    for p in range(NP):
        S_ref[p] = S[p]
        ps = slice(p * P, (p + 1) * P)
        o = jnp.concatenate(outs[p], axis=0)
        mean = seg_sum(o) * (1.0 / C_DIM)
        d = o - mean
        var = seg_sum(d * d) * (1.0 / C_DIM)
        on = d * lax.rsqrt(var + C_GN_EPS) * gnw_ref[:, ps] + gnb_ref[:, ps]
        bonus = seg_sum(r[:, ps] * k[:, ps] * rk_ref[:, ps]) * v[:, ps]
        o_ref[:, ps] = ((on + bonus) * gate[:, ps]).astype(o_ref.dtype)


def _rwkv(cols, mu_r, mu_k, mu_v, mu_sm, w0, w2p, a0, a2p, g2, k_k, k_a, r_k, gn_w, gn_b, Bn, S, tS=256):
    T = cols.shape[0]
    nS = S // tS
    W = C_WIDTH
    P = 2 * C_DIM

    def col(off, width):
        return pl.BlockSpec((tS, width), lambda b, s: (b * nS + s, off // width))

    def par(rows, width=W):
        return pl.BlockSpec((rows, width), lambda b, s: (0, 0))

    return pl.pallas_call(
        functools.partial(_rwkv_body, tS=tS),
        grid=(Bn, nS),
        in_specs=[col(P_CR, W), col(P_CK, W), col(P_CV, W), col(P_SM, SM_W), par(1), par(1), par(1),
                  par(1, SM_W), par(1), par(LANES), par(1), par(LANES), par(C_GATE_LORA), par(1), par(1),
                  par(1), par(1), par(1)],
        out_specs=pl.BlockSpec((tS, W), lambda b, s: (b * nS + s, 0)),
        out_shape=jax.ShapeDtypeStruct((T, W), BF16),
        scratch_shapes=[pltpu.VMEM((W // P, P, P), F32)] + [pltpu.VMEM((tS + 8, W), F32)] * 3
                       + [pltpu.VMEM((tS + 8, SM_W), F32)],
        compiler_params=_cparams(("parallel", "arbitrary")),
        name="rwkv",
    )(cols, cols, cols, cols, mu_r, mu_k, mu_v, mu_sm, w0, w2p, a0, a2p, g2, k_k, k_a, r_k, gn_w, gn_b)


def _merge_body(x_ref, ya_ref, yb_ref, yc_ref, g_ref, pa_ref, pb_ref, pc_ref, wo_ref, o_ref):
    D = x_ref.shape[1]
    merged = (jax.nn.sigmoid(g_ref[:, 0:D]) * jnp.dot(ya_ref[...], pa_ref[...], preferred_element_type=F32)
              + jax.nn.sigmoid(g_ref[:, D:2 * D]) * jnp.dot(yb_ref[...], pb_ref[...], preferred_element_type=F32)
              + jax.nn.sigmoid(g_ref[:, 2 * D:]) * jnp.dot(yc_ref[...], pc_ref[...], preferred_element_type=F32))
    o_ref[...] = x_ref[...] + jnp.dot(merged.astype(BF16), wo_ref[...], preferred_element_type=F32)


def _merge(x, y_a, y_b, y_c, cols, proj_a, proj_b, proj_c, w_out, l, tm=256):
    T, D = x.shape

    def rows(width):
        return pl.BlockSpec((tm, width), lambda i: (i, 0))

    def resident(kdim):
        return pl.BlockSpec((None, kdim, D), lambda i: (l, 0, 0), pipeline_mode=pl.Buffered(1))

    return pl.pallas_call(
        _merge_body,
        grid=(T // tm,),
        in_specs=[rows(D), rows(A_OUT), rows(B_WIDTH), rows(C_WIDTH),
                  pl.BlockSpec((tm, 3 * D), lambda i: (i, P_G // (3 * D))),
                  resident(A_OUT), resident(B_WIDTH), resident(C_WIDTH), resident(D)],
        out_specs=rows(D),
        out_shape=jax.ShapeDtypeStruct((T, D), F32),
        compiler_params=_cparams(("parallel",)),
        name="merge",
    )(x, y_a, y_b, y_c, cols, proj_a, proj_b, proj_c, w_out)


def _cast_body(w_ref, o_ref):
    o_ref[...] = w_ref[...].astype(BF16)


def _to_bf16(w):
    L, R, Cw = w.shape
    tr = 128 if Cw > 4096 else 256
    spec = pl.BlockSpec((None, tr, Cw), lambda l, i: (l, i, 0))
    return pl.pallas_call(
        _cast_body,
        grid=(L, R // tr),
        in_specs=[spec],
        out_specs=spec,
        out_shape=jax.ShapeDtypeStruct(w.shape, BF16),
        compiler_params=_cparams(("parallel", "parallel")),
        name="cast_bf16",
    )(w)


def _pack_w_in(w):
    a_cols = 3 * A_HEADS * HEAD_A
    b_ab = a_cols + 4 * B_WIDTH
    c0 = b_ab + 2 * B_HEADS
    c_xw = c0 + 3 * C_WIDTH
    c_xa = c_xw + C_DECAY_LORA
    c_xg = c_xa + C_ICLR_LORA
    g0 = c_xg + C_GATE_LORA
    wt = jnp.swapaxes(w, 1, 2)
    z = lambda n: jnp.zeros((w.shape[0], n, w.shape[1]), w.dtype)
    return jnp.concatenate([
        wt[:, g0:], wt[:, a_cols:b_ab], wt[:, c0:c_xw], wt[:, 0:a_cols],
        wt[:, c_xw:c_xa], wt[:, b_ab:c0], z(16), wt[:, c_xa:c_xg], z(32), wt[:, c_xg:g0],
    ], axis=1).astype(BF16)


def _pad_rows(w, n):
    return jnp.concatenate([w, jnp.zeros((n - w.shape[0], w.shape[1]), w.dtype)], axis=0)


def _mixers(x, cols, l, Bn, S, b_conv, b_a_log, b_dt_bias, b_norm, c_mu, c_w0, c_w2, c_a0, c_a2, c_g2,
            c_k_k, c_k_a, c_r_k, c_gn_w, c_gn_b):
    y_a = _attention(cols, Bn, S)
    y_b = _gdn(cols, b_conv[l], b_a_log[l], b_dt_bias[l], b_norm[l], Bn, S)
    W = C_WIDTH
    mu = c_mu[l]
    z = lambda n: jnp.zeros((n,), F32)
    mu_sm = jnp.concatenate([mu[3 * W:3 * W + C_DECAY_LORA], z(32),
                             mu[3 * W + C_DECAY_LORA:3 * W + C_DECAY_LORA + C_ICLR_LORA], z(32),
                             mu[3 * W + C_DECAY_LORA + C_ICLR_LORA:]])
    row = lambda t: t.reshape(1, -1)
    y_c = _rwkv(cols, row(mu[0:W]), row(mu[W:2 * W]), row(mu[2 * W:3 * W]), row(mu_sm),
                row(c_w0[l]), _pad_rows(c_w2[l], LANES).astype(BF16), row(c_a0[l]),
                _pad_rows(c_a2[l], LANES).astype(BF16), c_g2[l].astype(BF16),
                row(c_k_k[l]), row(c_k_a[l]), row(c_r_k[l]), row(c_gn_w[l]), row(c_gn_b[l]), Bn, S)
    return y_a, y_b, y_c


def kernel(x, ffn1_norm, ffn1_w_gu, ffn1_w_down, mix_norm, w_in, b_conv, b_a_log, b_dt_bias, b_norm,
           c_mu, c_w0, c_w2, c_a0, c_a2, c_g2, c_k_k, c_k_a, c_r_k, c_gn_w, c_gn_b,
           proj_a, proj_b, proj_c, w_out, ffn2_norm, ffn2_w_gu, ffn2_w_down, final_norm):
    Bn, S, D = x.shape
    depth = w_in.shape[0]
    x = x.reshape(Bn * S, D)
    f1_gu, f1_dn, f2_gu, f2_dn = (_to_bf16(w) for w in (ffn1_w_gu, ffn1_w_down, ffn2_w_gu, ffn2_w_down))
    p_a, p_b, p_c, w_o = (_to_bf16(w) for w in (proj_a, proj_b, proj_c, w_out))
    w_packed = _pack_w_in(w_in)
    for l in range(depth):
        x = _ffn(x, ffn1_norm[l], f1_gu, f1_dn, l)
        cols = _inproj(x, mix_norm[l], w_packed, l)
        y_a, y_b, y_c = _mixers(x, cols, l, Bn, S, b_conv, b_a_log, b_dt_bias, b_norm, c_mu, c_w0,
                                c_w2, c_a0, c_a2, c_g2, c_k_k, c_k_a, c_r_k, c_gn_w, c_gn_b)
        x = _merge(x, y_a, y_b, y_c, cols, p_a, p_b, p_c, w_o, l)
        x = _ffn(x, ffn2_norm[l], f2_gu, f2_dn, l, out_norm=final_norm if l == depth - 1 else None)
    return x.reshape(Bn, S, D)
```

```python
import functools

import jax
import jax.numpy as jnp
from jax import lax
from jax.experimental import pallas as pl
from jax.experimental.pallas import tpu as pltpu

F32 = jnp.float32
BF16 = jnp.bfloat16

D_MODEL = 2048
D_FF = 5632
A_DILATIONS = (1, 4, 16)
A_HPG = 4
A_HEADS = 12
HEAD_A = 128
A_OUT = A_HPG * HEAD_A
A_STEPS = 128
ALIBI_MAX_BIAS = 8.0
B_HEADS = 8
B_DIM = 128
B_WIDTH = B_HEADS * B_DIM
B_CONV = 4
C_DIM = 64
C_HEADS = 16
C_WIDTH = C_HEADS * C_DIM
C_DECAY_LORA = 96
C_ICLR_LORA = 96
C_GATE_LORA = 256
C_GN_EPS = 64e-5
C_DECAY_SCALE = 0.6065306597126334
NORM_EPS = 1e-6
L2_EPS = 1e-6
NEG_INF = -1e30
LANES = 128
CHUNK = 64
GROUP = 4 * CHUNK

P_G = 0
P_BQ, P_BK, P_BV, P_BZ = 6144, 7168, 8192, 9216
P_CR, P_CK, P_CV = 10240, 11264, 12288
P_AQ, P_AK, P_AV = 13312, 14848, 16384
P_SM = 17920
P_COLS = 18432
SM_W = 512
SM_ALPHA, SM_BETA, SM_XA, SM_XG = 96, 104, 128, 256

VMEM_LIMIT = 56 * 1024 * 1024


def _cparams(sem):
    return pltpu.CompilerParams(dimension_semantics=sem, vmem_limit_bytes=VMEM_LIMIT)


def _mm(a, b):
    return jnp.dot(a.astype(BF16), b.astype(BF16), preferred_element_type=F32)


def _mm_nt(a, b):
    return lax.dot_general(a.astype(BF16), b.astype(BF16), (((1,), (1,)), ((), ())),
                           preferred_element_type=F32)


def _mm_tn(a, b):
    return lax.dot_general(a.astype(BF16), b.astype(BF16), (((0,), (0,)), ((), ())),
                           preferred_element_type=F32)


def _bmm(a, b):
    return jnp.einsum("bmk,bkn->bmn", a.astype(BF16), b.astype(BF16), preferred_element_type=F32)


def _bmm_nt(a, b):
    return jnp.einsum("bmk,bnk->bmn", a.astype(BF16), b.astype(BF16), preferred_element_type=F32)


def _bmm_tn(a, b):
    return jnp.einsum("bkm,bkn->bmn", a.astype(BF16), b.astype(BF16), preferred_element_type=F32)


def _silu(x):
    return x * jax.nn.sigmoid(x)


def _softplus(x):
    return jnp.maximum(x, 0.0) + jnp.log1p(jnp.exp(-jnp.abs(x)))


def _rms_rows(x, g):
    ms = jnp.mean(x * x, axis=-1, keepdims=True)
    return x * lax.rsqrt(ms + NORM_EPS) * g


def _ffn_body(x_ref, g_ref, wg_ref, wu_ref, wd_ref, *rest, with_out_norm):
    o_ref, h_ref = rest[-2], rest[-1]

    @pl.when(pl.program_id(1) == 0)
    def _():
        x = x_ref[...]
        h_ref[...] = _rms_rows(x, g_ref[...]).astype(BF16)
        o_ref[...] = x

    h = h_ref[...]
    gate = jnp.dot(h, wg_ref[...], preferred_element_type=F32)
    up = jnp.dot(h, wu_ref[...], preferred_element_type=F32)
    act = (0.5 * _silu(gate) * up).astype(BF16)
    o_ref[...] += jnp.dot(act, wd_ref[...], preferred_element_type=F32)

    if with_out_norm:
        @pl.when(pl.program_id(1) == pl.num_programs(1) - 1)
        def _():
            o_ref[...] = _rms_rows(o_ref[...], rest[0][...])


def _ffn(x, g, w_gu, w_down, l, out_norm=None, tm=1024, tf=512):
    T, D = x.shape
    dff = w_down.shape[1]
    nj = dff // tf
    vec = pl.BlockSpec((1, D), lambda i, j: (0, 0))
    extra = [] if out_norm is None else [out_norm.reshape(1, D)]
    return pl.pallas_call(
        functools.partial(_ffn_body, with_out_norm=out_norm is not None),
        grid=(T // tm, nj),
        in_specs=[
            pl.BlockSpec((tm, D), lambda i, j: (i, 0)),
            vec,
            pl.BlockSpec((None, D, tf), lambda i, j: (l, 0, j)),
            pl.BlockSpec((None, D, tf), lambda i, j: (l, 0, j + nj)),
            pl.BlockSpec((None, tf, D), lambda i, j: (l, j, 0)),
        ] + [vec] * len(extra),
        out_specs=pl.BlockSpec((tm, D), lambda i, j: (i, 0)),
        out_shape=jax.ShapeDtypeStruct((T, D), F32),
        scratch_shapes=[pltpu.VMEM((tm, D), BF16)],
        compiler_params=_cparams(("parallel", "arbitrary")),
        name="ffn",
    )(x, g.reshape(1, D), w_gu, w_gu, w_down, *extra)


def _inproj_body(x_ref, g_ref, w_ref, o_ref, h_ref):
    @pl.when(pl.program_id(1) == 0)
    def _():
        h_ref[...] = _rms_rows(x_ref[...], g_ref[...]).astype(BF16)

    o_ref[...] = _mm_nt(h_ref[...], w_ref[...])


def _inproj(x, g, w, l, tm=1024, tn=1024):
    T, D = x.shape
    N = w.shape[1]
    return pl.pallas_call(
        _inproj_body,
        grid=(T // tm, N // tn),
        in_specs=[
            pl.BlockSpec((tm, D), lambda i, j: (i, 0)),
            pl.BlockSpec((1, D), lambda i, j: (0, 0)),
            pl.BlockSpec((None, tn, D), lambda i, j: (l, j, 0)),
        ],
        out_specs=pl.BlockSpec((tm, tn), lambda i, j: (i, j)),
        out_shape=jax.ShapeDtypeStruct((T, N), F32),
        scratch_shapes=[pltpu.VMEM((tm, D), BF16)],
        compiler_params=_cparams(("parallel", "arbitrary")),
        name="inproj",
    )(x, g.reshape(1, D), w)


def _attn_body(slope_ref, *refs, S):
    n_g = len(A_DILATIONS)
    ins, o_ref, scr = refs[:3 * n_g], refs[3 * n_g], refs[3 * n_g + 1:]
    blk = A_STEPS
    slot = pl.program_id(1)
    qi = lax.broadcasted_iota(jnp.int32, (blk, 2 * blk), 0)
    ki = lax.broadcasted_iota(jnp.int32, (blk, 2 * blk), 1)
    delta2 = qi + blk - ki
    valid2 = (delta2 >= 0) & (delta2 <= A_STEPS)
    dist2 = delta2.astype(F32)
    delta1 = delta2[:, blk:]
    valid1 = delta1 >= 0
    dist1 = delta1.astype(F32)
    for g, d in enumerate(A_DILATIONS):
        q_ref, k_ref, v_ref = ins[3 * g:3 * g + 3]
        o_s, m_s, l_s = scr[3 * g], scr[3 * g + 1], scr[3 * g + 2]
        slope = slope_ref[g * A_HPG + slot] * float(d)
        nj = S // (d * blk)
        rows = [pl.ds(r + d * blk * j, blk, stride=d) if d > 1 else pl.ds(blk * j, blk)
                for r in range(d) for j in range(nj)]
        nblk = len(rows)
        take = lambda ref: [ref[rw, :].astype(BF16) for rw in rows]
        q = jnp.stack(take(q_ref), axis=0)
        k_cur, v_cur = take(k_ref), take(v_ref)
        if nj > 1:
            none = jnp.zeros((blk, HEAD_A), BF16)
            prev = lambda t: [none if i % nj == 0 else t[i - 1] for i in range(nblk)]
            k = jnp.stack([jnp.concatenate(pc, axis=0) for pc in zip(prev(k_cur), k_cur)], axis=0)
            v = jnp.stack([jnp.concatenate(pc, axis=0) for pc in zip(prev(v_cur), v_cur)], axis=0)
            first = lax.broadcasted_iota(jnp.int32, (nblk, 1, 1), 0) % nj == 0
            valid = valid2 & ((ki >= blk) | jnp.logical_not(first))
            dist = dist2
        else:
            k, v, valid, dist = jnp.stack(k_cur, axis=0), jnp.stack(v_cur, axis=0), valid1, dist1
        s = _bmm_nt(q, k) * (HEAD_A ** -0.5)
        s = jnp.where(valid, s - slope * dist, NEG_INF)
        m = jnp.max(s, axis=-1, keepdims=True)
        p = jnp.exp(s - m)
        l = jnp.sum(p, axis=-1, keepdims=True)
        o = _bmm(p, v)
        for i, rw in enumerate(rows):
            o_s[rw, :] = o[i]
            m_s[rw, :] = jnp.broadcast_to(m[i], (blk, HEAD_A))
            l_s[rw, :] = jnp.broadcast_to(l[i], (blk, HEAD_A))
    m0, m1, m2 = scr[1][...], scr[4][...], scr[7][...]
    m = jnp.maximum(jnp.maximum(m0, m1), m2)
    e0, e1, e2 = jnp.exp(m0 - m), jnp.exp(m1 - m), jnp.exp(m2 - m)
    den = e0 * scr[2][...] + e1 * scr[5][...] + e2 * scr[8][...]
    o_ref[...] = ((e0 * scr[0][...] + e1 * scr[3][...] + e2 * scr[6][...]) / den).astype(o_ref.dtype)


def _attention(cols, Bn, S):
    T = cols.shape[0]
    slopes = jnp.asarray([2.0 ** (-ALIBI_MAX_BIAS * (h + 1.0) / A_HEADS) for h in range(A_HEADS)], F32)

    def col(off, g):
        return pl.BlockSpec((S, HEAD_A), lambda b, s: (b, off // HEAD_A + g * A_HPG + s))

    in_specs = [pl.BlockSpec(memory_space=pltpu.SMEM)]
    for g in range(len(A_DILATIONS)):
        in_specs += [col(P_AQ, g), col(P_AK, g), col(P_AV, g)]
    return pl.pallas_call(
        functools.partial(_attn_body, S=S),
        grid=(Bn, A_HPG),
        in_specs=in_specs,
        out_specs=pl.BlockSpec((S, HEAD_A), lambda b, s: (b, s)),
        out_shape=jax.ShapeDtypeStruct((T, A_OUT), BF16),
        scratch_shapes=[pltpu.VMEM((S, HEAD_A), F32)] * (3 * len(A_DILATIONS)),
        compiler_params=_cparams(("parallel", "parallel")),
        name="attention",
    )(slopes, *([cols] * (3 * len(A_DILATIONS))))


def _group_masks(ng):
    ri = lax.broadcasted_iota(jnp.int32, (ng, GROUP, GROUP), 1)
    ci = lax.broadcasted_iota(jnp.int32, (ng, GROUP, GROUP), 2)
    same = (ri ^ ci) < CHUNK
    return same, same & (ri >= ci), same & (ri > ci)


def _neumann_inverse_bd(pw, same):
    C = CHUNK
    ng = pw.shape[0]
    spread = lambda x: jnp.where(same, jnp.concatenate([x] * (GROUP // C), axis=1), 0.0)
    ri = lax.broadcasted_iota(jnp.int32, (ng, C, GROUP), 1)
    ci = lax.broadcasted_iota(jnp.int32, (ng, C, GROUP), 2)
    tinv = jnp.where(ri == (ci & (C - 1)), 1.0, 0.0) + pw
    pw = _bmm(pw, spread(pw))
    for _ in range(4):
        both = _bmm(jnp.concatenate([pw, tinv], axis=1), spread(pw))
        pw = both[:, :C]
        tinv = tinv + both[:, C:]
    tinv = tinv + _bmm(tinv, spread(pw))
    return spread(tinv)


def _gdn_body(alog_ref, dtb_ref, q_ref, k_ref, v_ref, z_ref, sm_ref, cw_ref, nw_ref,
              o_ref, S_ref, bq, bk, bv, *, tS):
    C, H, D, G = CHUNK, B_HEADS, B_DIM, GROUP
    n_c = tS // C
    nb = H * n_c
    ng = nb * C // G
    s = pl.program_id(1)

    @pl.when(s == 0)
    def _():
        S_ref[...] = jnp.zeros_like(S_ref)
        for buf in (bq, bk, bv):
            buf[0:8, :] = jnp.zeros((8, B_WIDTH), F32)

    def conv_silu(raw_ref, buf, off):
        x = raw_ref[...]
        buf[8:8 + tS, :] = x
        full = buf[...]
        y = cw_ref[B_CONV - 1:B_CONV, off:off + B_WIDTH] * x
        for j in range(B_CONV - 1):
            y = y + cw_ref[j:j + 1, off:off + B_WIDTH] * pltpu.roll(full, B_CONV - 1 - j, axis=0)[8:8 + tS, :]
        buf[0:8, :] = buf[tS:tS + 8, :]
        return _silu(y)

    def heads(x):
        return jnp.stack([x[:, h * D:(h + 1) * D] for h in range(H)], axis=0).reshape(nb, C, D)

    q = heads(conv_silu(q_ref, bq, 0))
    k = heads(conv_silu(k_ref, bk, B_WIDTH))
    v = heads(conv_silu(v_ref, bv, 2 * B_WIDTH))
    q = q * (lax.rsqrt(jnp.sum(q * q, axis=-1, keepdims=True) + L2_EPS) * (D ** -0.5))
    k = k * lax.rsqrt(jnp.sum(k * k, axis=-1, keepdims=True) + L2_EPS)

    sm = sm_ref[:, 0:LANES]
    lane = lax.broadcasted_iota(jnp.int32, sm.shape, 1)
    lane1 = lax.broadcasted_iota(jnp.int32, (1, LANES), 1)
    dtb = jnp.zeros((1, LANES), F32)
    alog = jnp.zeros((1, LANES), F32)
    for h in range(H):
        dtb = jnp.where(lane1 == SM_ALPHA + h, dtb_ref[h], dtb)
        alog = jnp.where(lane1 == SM_ALPHA + h, alog_ref[h], alog)
    ld_all = -jnp.exp(alog) * _softplus(sm + dtb)
    beta_all = jax.nn.sigmoid(sm)
    ld, beta = [], []
    for h in range(H):
        ld.append(jnp.sum(jnp.where(lane == SM_ALPHA + h, ld_all, 0.0), axis=-1, keepdims=True))
        beta.append(jnp.sum(jnp.where(lane == SM_BETA + h, beta_all, 0.0), axis=-1, keepdims=True))
    ld = jnp.stack(ld, axis=0).reshape(ng, G, 1)
    beta = jnp.stack(beta, axis=0).reshape(ng, G, 1)

    same, tril, strict = _group_masks(ng)
    ri = lax.broadcasted_iota(jnp.int32, (ng, G, G), 1)
    ci = lax.broadcasted_iota(jnp.int32, (ng, G, G), 2)
    grp = lambda t: t.reshape(ng, G, t.shape[-1])
    blk = lambda t: t.reshape(nb, C, t.shape[-1])
    q, k, v = grp(q), grp(k), grp(v)
    LD = jnp.broadcast_to(ld, (ng, G, G))
    g_row = jnp.sum(jnp.where(same & (ri <= ci), LD, 0.0), axis=1, keepdims=True)
    ld_row = jnp.sum(jnp.where(ri == ci, LD, 0.0), axis=1, keepdims=True)
    g_col = jnp.sum(jnp.where(tril, jnp.broadcast_to(ld_row, (ng, G, G)), 0.0), axis=2, keepdims=True)
    g_last = blk(g_col)[:, C - 1:C, :]
    decay = jnp.exp(jnp.where(tril, g_col - g_row, NEG_INF))
    eg = jnp.exp(g_col)
    kb = k * beta
    vb = v * beta
    kq = _bmm_nt(jnp.concatenate([kb, q], axis=1), k)
    m = jnp.where(strict, kq[:, :G] * decay, 0.0)
    att = jnp.where(tril, kq[:, G:] * decay, 0.0)
    m_cat = m[:, 0:C]
    for i in range(1, G // C):
        m_cat = m_cat + m[:, i * C:(i + 1) * C]
    tinv = _neumann_inverse_bd(-m_cat, same)
    wu = _bmm(tinv, jnp.concatenate([kb * eg, vb], axis=2))
    k_dec = blk(k) * jnp.exp(g_last - blk(g_col))
    gz = _bmm_tn(k_dec, blk(wu))
    aw = _bmm(att, wu)
    q_eff = blk(q * eg - aw[:, :, :D])
    o0 = blk(aw[:, :, D:])
    cd = jnp.exp(g_last)
    S = [S_ref[h] for h in range(H)]
    outs = [[None] * n_c for _ in range(H)]
    for c in range(n_c):
        for h in range(H):
            i = h * n_c + c
            outs[h][c] = _mm(q_eff[i], S[h]) + o0[i]
            S[h] = S[h] * cd[i] - _mm(gz[i, :, :D], S[h]) + gz[i, :, D:]
    nw = nw_ref[...]
    for h in range(H):
        S_ref[h] = S[h]
        o = jnp.concatenate(outs[h], axis=0)
        hs = slice(h * D, (h + 1) * D)
        o_ref[:, hs] = (_rms_rows(o, nw) * _silu(z_ref[:, hs])).astype(o_ref.dtype)


def _gdn(cols, b_conv, a_log, dt_bias, norm_w, Bn, S, tS=256):
    T = cols.shape[0]
    nS = S // tS
    W = B_WIDTH

    def col(off):
        return pl.BlockSpec((tS, W), lambda b, s: (b * nS + s, off // W))

    smem = pl.BlockSpec(memory_space=pltpu.SMEM)
    return pl.pallas_call(
        functools.partial(_gdn_body, tS=tS),
        grid=(Bn, nS),
        in_specs=[smem, smem, col(P_BQ), col(P_BK), col(P_BV), col(P_BZ),
                  pl.BlockSpec((tS, SM_W), lambda b, s: (b * nS + s, P_SM // SM_W)),
                  pl.BlockSpec((B_CONV, 3 * W), lambda b, s: (0, 0)),
                  pl.BlockSpec((1, B_DIM), lambda b, s: (0, 0))],
        out_specs=pl.BlockSpec((tS, W), lambda b, s: (b * nS + s, 0)),
        out_shape=jax.ShapeDtypeStruct((T, W), BF16),
        scratch_shapes=[pltpu.VMEM((B_HEADS, B_DIM, B_DIM), F32)] + [pltpu.VMEM((tS + 8, W), F32)] * 3,
        compiler_params=_cparams(("parallel", "arbitrary")),
        name="gdn",
    )(a_log, dt_bias, cols, cols, cols, cols, cols, b_conv, norm_w.reshape(1, B_DIM))


def _rwkv_body(r_ref, k_ref, v_ref, sm_ref, mur_ref, muk_ref, muv_ref, musm_ref,
               w0_ref, w2_ref, a0_ref, a2_ref, g2_ref, kkw_ref, kaw_ref, rk_ref, gnw_ref, gnb_ref,
               o_ref, S_ref, br, bk, bv, bsm, *, tS):
    C = CHUNK
    P = 2 * C_DIM
    NP = C_WIDTH // P
    s = pl.program_id(1)

    @pl.when(s == 0)
    def _():
        S_ref[...] = jnp.zeros_like(S_ref)
        for buf in (br, bk, bv, bsm):
            buf[0:8, :] = jnp.zeros((8, buf.shape[1]), F32)

    def shift(x_ref, buf, mu_ref):
        x = x_ref[...]
        buf[8:8 + tS, :] = x
        prev = pltpu.roll(buf[...], 1, axis=0)[8:8 + tS, :]
        buf[0:8, :] = buf[tS:tS + 8, :]
        return x + (prev - x) * mu_ref[...]

    r = shift(r_ref, br, mur_ref)
    k = shift(k_ref, bk, muk_ref)
    v = shift(v_ref, bv, muv_ref)
    sm = shift(sm_ref, bsm, musm_ref)
    xw, xa, xg = sm[:, 0:LANES], sm[:, SM_XA:SM_XA + LANES], sm[:, SM_XG:SM_XG + C_GATE_LORA]
    lw = -C_DECAY_SCALE * jax.nn.sigmoid(w0_ref[...] + _mm(jnp.tanh(xw), w2_ref[...]))
    a = jax.nn.sigmoid(a0_ref[...] + _mm(xa, a2_ref[...]))
    gate = _mm(jax.nn.sigmoid(xg), g2_ref[...])
    kkraw = k * kkw_ref[...]
    k = k * (1.0 + (a - 1.0) * kaw_ref[...])

    n_c = tS // C
    nb = NP * n_c
    ng = nb * 2 * C // GROUP
    m0 = lax.broadcasted_iota(jnp.int32, (1, P), 1) < C_DIM
    rs = lax.broadcasted_iota(jnp.int32, (tS, tS), 0)
    cs = lax.broadcasted_iota(jnp.int32, (tS, tS), 1)
    tril_b = jnp.where((rs >= cs) & ((rs ^ cs) < C), 1.0, 0.0).astype(BF16)
    r2 = lax.broadcasted_iota(jnp.int32, (1, 2 * C, 2 * C), 1)
    c2 = lax.broadcasted_iota(jnp.int32, (1, 2 * C, 2 * C), 2)
    strict2 = r2 > c2
    rt = lax.broadcasted_iota(jnp.int32, (1, C, 2 * C), 1)
    ct = lax.broadcasted_iota(jnp.int32, (1, C, 2 * C), 2)
    incl = rt >= (ct & (C - 1))

    def seg_sum(x):
        s0 = jnp.sum(jnp.where(m0, x, 0.0), axis=-1, keepdims=True)
        s1 = jnp.sum(jnp.where(m0, 0.0, x), axis=-1, keepdims=True)
        return jnp.where(m0, s0, s1)

    def pairs(x):
        return jnp.stack([x[:, p * P:(p + 1) * P] for p in range(NP)], axis=0).reshape(nb, C, P)

    def stack(x):
        return jnp.concatenate([jnp.where(m0, x, 0.0), jnp.where(m0, 0.0, x)], axis=1)

    lw_hi = lw.astype(BF16)
    lw_lo = (lw - lw_hi.astype(F32)).astype(BF16)
    cum2 = jnp.dot(tril_b, jnp.concatenate([lw_hi, lw_lo], axis=1), preferred_element_type=F32)
    cum = cum2[:, :C_WIDTH] + cum2[:, C_WIDTH:]
    kkraw = pairs(kkraw)
    kk = kkraw * lax.rsqrt(seg_sum(kkraw * kkraw) + L2_EPS)
    bv = kk * pairs(a)
    w_inc = pairs(jnp.exp(cum))
    w_inv = pairs(jnp.exp(-cum))
    w_end = w_inc[:, C - 1:C, :]
    r_t = pairs(r) * w_inc
    a_st = stack(-kk * pairs(jnp.exp(cum - lw)))
    b_st = stack(bv * w_inv)
    k_st = stack(pairs(k) * w_inv)
    v_st = stack(pairs(v))
    aa = _bmm_nt(jnp.concatenate([a_st, r_t], axis=1), jnp.concatenate([b_st, k_st], axis=1))
    a_ab = jnp.where(strict2, aa[:, :2 * C, :2 * C], 0.0)
    a_ak = jnp.where(strict2, aa[:, :2 * C, 2 * C:], 0.0)
    a_rb = jnp.where(incl, aa[:, 2 * C:, :2 * C], 0.0)
    a_rk = jnp.where(incl, aa[:, 2 * C:, 2 * C:], 0.0)
    a4 = a_ab.reshape(ng, 2, 2 * C, 2 * C)
    a_cat = jnp.concatenate([a4[:, 0, :C] + a4[:, 0, C:], a4[:, 1, :C] + a4[:, 1, C:]], axis=2)
    tinv = _neumann_inverse_bd(a_cat, _group_masks(ng)[0])
    x = jnp.concatenate([a_st, _bmm(a_ak, v_st)], axis=2)
    wu = _bmm(tinv, x.reshape(ng, GROUP, 2 * P)).reshape(nb, 2 * C, 2 * P)
    rw = _bmm(a_rb, wu)
    r_eff = r_t + rw[:, :, :P]
    o0 = rw[:, :, P:] + _bmm(a_rk, v_st)
    mz = _bmm_tn(wu, b_st * w_end)
    z = mz[:, P:, :] + _bmm_tn(v_st, k_st * w_end)
    S = [S_ref[p] for p in range(NP)]
    outs = [[None] * n_c for _ in range(NP)]
    for c in range(n_c):
        for p in range(NP):
            i = p * n_c + c
            outs[p][c] = _mm_nt(r_eff[i], S[p]) + o0[i]
            S[p] = S[p] * w_end[i] + _mm(S[p], mz[i, :P, :]) + z[i]
    for p in range(NP):
        S_ref[p] = S[p]
        ps = slice(p * P, (p + 1) * P)
        o = jnp.concatenate(outs[p], axis=0)
        mean = seg_sum(o) * (1.0 / C_DIM)
        d = o - mean
        var = seg_sum(d * d) * (1.0 / C_DIM)
        on = d * lax.rsqrt(var + C_GN_EPS) * gnw_ref[:, ps] + gnb_ref[:, ps]
        bonus = seg_sum(r[:, ps] * k[:, ps] * rk_ref[:, ps]) * v[:, ps]
        o_ref[:, ps] = ((on + bonus) * gate[:, ps]).astype(o_ref.dtype)


def _rwkv(cols, mu_r, mu_k, mu_v, mu_sm, w0, w2p, a0, a2p, g2, k_k, k_a, r_k, gn_w, gn_b, Bn, S, tS=256):
    T = cols.shape[0]
    nS = S // tS
    W = C_WIDTH
    P = 2 * C_DIM

    def col(off, width):
        return pl.BlockSpec((tS, width), lambda b, s: (b * nS + s, off // width))

    def par(rows, width=W):
        return pl.BlockSpec((rows, width), lambda b, s: (0, 0))

    return pl.pallas_call(
        functools.partial(_rwkv_body, tS=tS),
        grid=(Bn, nS),
        in_specs=[col(P_CR, W), col(P_CK, W), col(P_CV, W), col(P_SM, SM_W), par(1), par(1), par(1),
                  par(1, SM_W), par(1), par(LANES), par(1), par(LANES), par(C_GATE_LORA), par(1), par(1),
                  par(1), par(1), par(1)],
        out_specs=pl.BlockSpec((tS, W), lambda b, s: (b * nS + s, 0)),
        out_shape=jax.ShapeDtypeStruct((T, W), BF16),
        scratch_shapes=[pltpu.VMEM((W // P, P, P), F32)] + [pltpu.VMEM((tS + 8, W), F32)] * 3
                       + [pltpu.VMEM((tS + 8, SM_W), F32)],
        compiler_params=_cparams(("parallel", "arbitrary")),
        name="rwkv",
    )(cols, cols, cols, cols, mu_r, mu_k, mu_v, mu_sm, w0, w2p, a0, a2p, g2, k_k, k_a, r_k, gn_w, gn_b)


def _merge_body(x_ref, ya_ref, yb_ref, yc_ref, g_ref, pa_ref, pb_ref, pc_ref, wo_ref, o_ref):
    D = x_ref.shape[1]
    merged = (jax.nn.sigmoid(g_ref[:, 0:D]) * jnp.dot(ya_ref[...], pa_ref[...], preferred_element_type=F32)
              + jax.nn.sigmoid(g_ref[:, D:2 * D]) * jnp.dot(yb_ref[...], pb_ref[...], preferred_element_type=F32)
              + jax.nn.sigmoid(g_ref[:, 2 * D:]) * jnp.dot(yc_ref[...], pc_ref[...], preferred_element_type=F32))
    o_ref[...] = x_ref[...] + jnp.dot(merged.astype(BF16), wo_ref[...], preferred_element_type=F32)


def _merge(x, y_a, y_b, y_c, cols, proj_a, proj_b, proj_c, w_out, l, tm=256):
    T, D = x.shape

    def rows(width):
        return pl.BlockSpec((tm, width), lambda i: (i, 0))

    def resident(kdim):
        return pl.BlockSpec((None, kdim, D), lambda i: (l, 0, 0), pipeline_mode=pl.Buffered(1))

    return pl.pallas_call(
        _merge_body,
        grid=(T // tm,),
        in_specs=[rows(D), rows(A_OUT), rows(B_WIDTH), rows(C_WIDTH),
                  pl.BlockSpec((tm, 3 * D), lambda i: (i, P_G // (3 * D))),
                  resident(A_OUT), resident(B_WIDTH), resident(C_WIDTH), resident(D)],
        out_specs=rows(D),
        out_shape=jax.ShapeDtypeStruct((T, D), F32),
        compiler_params=_cparams(("parallel",)),
        name="merge",
    )(x, y_a, y_b, y_c, cols, proj_a, proj_b, proj_c, w_out)


def _cast_body(w_ref, o_ref):
    o_ref[...] = w_ref[...].astype(BF16)


def _to_bf16(w):
    L, R, Cw = w.shape
    tr = 128 if Cw > 4096 else 256
    spec = pl.BlockSpec((None, tr, Cw), lambda l, i: (l, i, 0))
    return pl.pallas_call(
        _cast_body,
        grid=(L, R // tr),
        in_specs=[spec],
        out_specs=spec,
        out_shape=jax.ShapeDtypeStruct(w.shape, BF16),
        compiler_params=_cparams(("parallel", "parallel")),
        name="cast_bf16",
    )(w)


def _pack_body(w_ref, sm_ref, o_ref, *, n_big):
    t = pl.program_id(1)
    o_ref[...] = jnp.where(t >= n_big, sm_ref[...], w_ref[0]).astype(BF16)


def _pack_w_in(w, tr=512):
    L, D, _ = w.shape
    a_cols = 3 * A_HEADS * HEAD_A
    b_ab = a_cols + 4 * B_WIDTH
    c0 = b_ab + 2 * B_HEADS
    c_xw = c0 + 3 * C_WIDTH
    c_xa = c_xw + C_DECAY_LORA
    c_xg = c_xa + C_ICLR_LORA
    g0 = c_xg + C_GATE_LORA
    wt = jnp.swapaxes(w, 1, 2)
    z = lambda n: jnp.zeros((L, n, D), w.dtype)
    small = jnp.concatenate([wt[:, c_xw:c_xa], wt[:, b_ab:c0], z(16), wt[:, c_xa:c_xg], z(32), wt[:, c_xg:g0]],
                            axis=1)
    n_big = P_SM // tr
    n_small = SM_W // tr

    def src_row(t):
        r = t * tr
        off = jnp.where(r < P_BQ, g0 - P_G,
                        jnp.where(r < P_CR, a_cols - P_BQ, jnp.where(r < P_AQ, c0 - P_CR, 0 - P_AQ)))
        return pl.multiple_of(jnp.where(t < n_big, r + off, 0), 16)

    return pl.pallas_call(
        functools.partial(_pack_body, n_big=n_big),
        grid=(L, n_big + n_small),
        in_specs=[pl.BlockSpec((pl.Element(1), pl.Element(tr), pl.Element(D)), lambda l, t: (l, src_row(t), 0)),
                  pl.BlockSpec((None, tr, D), lambda l, t: (l, jnp.clip(t - n_big, 0, n_small - 1), 0))],
        out_specs=pl.BlockSpec((None, tr, D), lambda l, t: (l, t, 0)),
        out_shape=jax.ShapeDtypeStruct((L, P_COLS, D), BF16),
        compiler_params=_cparams(("parallel", "arbitrary")),
        name="pack_w_in",
    )(wt, small)


def _pad_rows(w, n):
    return jnp.concatenate([w, jnp.zeros((n - w.shape[0], w.shape[1]), w.dtype)], axis=0)


def _mixers(x, cols, l, Bn, S, b_conv, b_a_log, b_dt_bias, b_norm, c_mu, c_w0, c_w2, c_a0, c_a2, c_g2,
            c_k_k, c_k_a, c_r_k, c_gn_w, c_gn_b):
    y_a = _attention(cols, Bn, S)
    y_b = _gdn(cols, b_conv[l], b_a_log[l], b_dt_bias[l], b_norm[l], Bn, S)
    W = C_WIDTH
    mu = c_mu[l]
    z = lambda n: jnp.zeros((n,), F32)
    mu_sm = jnp.concatenate([mu[3 * W:3 * W + C_DECAY_LORA], z(32),
                             mu[3 * W + C_DECAY_LORA:3 * W + C_DECAY_LORA + C_ICLR_LORA], z(32),
                             mu[3 * W + C_DECAY_LORA + C_ICLR_LORA:]])
    row = lambda t: t.reshape(1, -1)
    y_c = _rwkv(cols, row(mu[0:W]), row(mu[W:2 * W]), row(mu[2 * W:3 * W]), row(mu_sm),
                row(c_w0[l]), _pad_rows(c_w2[l], LANES).astype(BF16), row(c_a0[l]),
                _pad_rows(c_a2[l], LANES).astype(BF16), c_g2[l].astype(BF16),
                row(c_k_k[l]), row(c_k_a[l]), row(c_r_k[l]), row(c_gn_w[l]), row(c_gn_b[l]), Bn, S)
    return y_a, y_b, y_c


def kernel(x, ffn1_norm, ffn1_w_gu, ffn1_w_down, mix_norm, w_in, b_conv, b_a_log, b_dt_bias, b_norm,
           c_mu, c_w0, c_w2, c_a0, c_a2, c_g2, c_k_k, c_k_a, c_r_k, c_gn_w, c_gn_b,
           proj_a, proj_b, proj_c, w_out, ffn2_norm, ffn2_w_gu, ffn2_w_down, final_norm):
    Bn, S, D = x.shape
    depth = w_in.shape[0]
    x = x.reshape(Bn * S, D)
    f1_gu, f1_dn, f2_gu, f2_dn = (_to_bf16(w) for w in (ffn1_w_gu, ffn1_w_down, ffn2_w_gu, ffn2_w_down))
    p_a, p_b, p_c, w_o = (_to_bf16(w) for w in (proj_a, proj_b, proj_c, w_out))
    w_packed = _pack_w_in(w_in)
    for l in range(depth):
        x = _ffn(x, ffn1_norm[l], f1_gu, f1_dn, l)
        cols = _inproj(x, mix_norm[l], w_packed, l)
        y_a, y_b, y_c = _mixers(x, cols, l, Bn, S, b_conv, b_a_log, b_dt_bias, b_norm, c_mu, c_w0,
                                c_w2, c_a0, c_a2, c_g2, c_k_k, c_k_a, c_r_k, c_gn_w, c_gn_b)
        x = _merge(x, y_a, y_b, y_c, cols, p_a, p_b, p_c, w_o, l)
        x = _ffn(x, ffn2_norm[l], f2_gu, f2_dn, l, out_norm=final_norm if l == depth - 1 else None)
    return x.reshape(Bn, S, D)
```

```python
import functools

import jax
import jax.numpy as jnp
from jax import lax
from jax.experimental import pallas as pl
from jax.experimental.pallas import tpu as pltpu

F32 = jnp.float32
BF16 = jnp.bfloat16

D_MODEL = 2048
D_FF = 5632
A_DILATIONS = (1, 4, 16)
A_HPG = 4
A_HEADS = 12
HEAD_A = 128
A_OUT = A_HPG * HEAD_A
A_STEPS = 128
ALIBI_MAX_BIAS = 8.0
B_HEADS = 8
B_DIM = 128
B_WIDTH = B_HEADS * B_DIM
B_CONV = 4
C_DIM = 64
C_HEADS = 16
C_WIDTH = C_HEADS * C_DIM
C_DECAY_LORA = 96
C_ICLR_LORA = 96
C_GATE_LORA = 256
C_GN_EPS = 64e-5
C_DECAY_SCALE = 0.6065306597126334
NORM_EPS = 1e-6
L2_EPS = 1e-6
NEG_INF = -1e30
LANES = 128
CHUNK = 64
GROUP = 4 * CHUNK

P_G = 0
P_BQ, P_BK, P_BV, P_BZ = 6144, 7168, 8192, 9216
P_CR, P_CK, P_CV = 10240, 11264, 12288
P_AQ, P_AK, P_AV = 13312, 14848, 16384
P_SM = 17920
P_COLS = 18432
SM_W = 512
SM_ALPHA, SM_BETA, SM_XA, SM_XG = 96, 104, 128, 256

VMEM_LIMIT = 56 * 1024 * 1024


def _cparams(sem):
    return pltpu.CompilerParams(dimension_semantics=sem, vmem_limit_bytes=VMEM_LIMIT)


def _mm(a, b):
    return jnp.dot(a.astype(BF16), b.astype(BF16), preferred_element_type=F32)


def _mm_nt(a, b):
    return lax.dot_general(a.astype(BF16), b.astype(BF16), (((1,), (1,)), ((), ())),
                           preferred_element_type=F32)


def _mm_tn(a, b):
    return lax.dot_general(a.astype(BF16), b.astype(BF16), (((0,), (0,)), ((), ())),
                           preferred_element_type=F32)


def _bmm(a, b):
    return jnp.einsum("bmk,bkn->bmn", a.astype(BF16), b.astype(BF16), preferred_element_type=F32)


def _bmm_nt(a, b):
    return jnp.einsum("bmk,bnk->bmn", a.astype(BF16), b.astype(BF16), preferred_element_type=F32)


def _bmm_tn(a, b):
    return jnp.einsum("bkm,bkn->bmn", a.astype(BF16), b.astype(BF16), preferred_element_type=F32)


def _silu(x):
    return x * jax.nn.sigmoid(x)


def _softplus(x):
    return jnp.maximum(x, 0.0) + jnp.log1p(jnp.exp(-jnp.abs(x)))


def _rms_rows(x, g):
    ms = jnp.mean(x * x, axis=-1, keepdims=True)
    return x * lax.rsqrt(ms + NORM_EPS) * g


def _ffn_body(x_ref, g_ref, wg_ref, wu_ref, wd_ref, *rest, with_out_norm):
    o_ref, h_ref = rest[-2], rest[-1]

    @pl.when(pl.program_id(1) == 0)
    def _():
        x = x_ref[...]
        h_ref[...] = _rms_rows(x, g_ref[...]).astype(BF16)
        o_ref[...] = x

    h = h_ref[...]
    gate = jnp.dot(h, wg_ref[...], preferred_element_type=F32)
    up = jnp.dot(h, wu_ref[...], preferred_element_type=F32)
    act = (0.5 * _silu(gate) * up).astype(BF16)
    o_ref[...] += jnp.dot(act, wd_ref[...], preferred_element_type=F32)

    if with_out_norm:
        @pl.when(pl.program_id(1) == pl.num_programs(1) - 1)
        def _():
            o_ref[...] = _rms_rows(o_ref[...], rest[0][...])


def _ffn(x, g, w_gu, w_down, l, out_norm=None, tm=1024, tf=512):
    T, D = x.shape
    dff = w_down.shape[1]
    nj = dff // tf
    vec = pl.BlockSpec((1, D), lambda i, j: (0, 0))
    extra = [] if out_norm is None else [out_norm.reshape(1, D)]
    return pl.pallas_call(
        functools.partial(_ffn_body, with_out_norm=out_norm is not None),
        grid=(T // tm, nj),
        in_specs=[
            pl.BlockSpec((tm, D), lambda i, j: (i, 0)),
            vec,
            pl.BlockSpec((None, D, tf), lambda i, j: (l, 0, j)),
            pl.BlockSpec((None, D, tf), lambda i, j: (l, 0, j + nj)),
            pl.BlockSpec((None, tf, D), lambda i, j: (l, j, 0)),
        ] + [vec] * len(extra),
        out_specs=pl.BlockSpec((tm, D), lambda i, j: (i, 0)),
        out_shape=jax.ShapeDtypeStruct((T, D), F32),
        scratch_shapes=[pltpu.VMEM((tm, D), BF16)],
        compiler_params=_cparams(("parallel", "arbitrary")),
        name="ffn",
    )(x, g.reshape(1, D), w_gu, w_gu, w_down, *extra)


def _inproj_body(x_ref, g_ref, w_ref, o_ref, h_ref):
    @pl.when(pl.program_id(1) == 0)
    def _():
        h_ref[...] = _rms_rows(x_ref[...], g_ref[...]).astype(BF16)

    o_ref[...] = _mm_nt(h_ref[...], w_ref[...])


def _inproj(x, g, w, l, tm=1024, tn=2048):
    T, D = x.shape
    N = w.shape[1]
    return pl.pallas_call(
        _inproj_body,
        grid=(T // tm, N // tn),
        in_specs=[
            pl.BlockSpec((tm, D), lambda i, j: (i, 0)),
            pl.BlockSpec((1, D), lambda i, j: (0, 0)),
            pl.BlockSpec((None, tn, D), lambda i, j: (l, j, 0)),
        ],
        out_specs=pl.BlockSpec((tm, tn), lambda i, j: (i, j)),
        out_shape=jax.ShapeDtypeStruct((T, N), F32),
        scratch_shapes=[pltpu.VMEM((tm, D), BF16)],
        compiler_params=_cparams(("parallel", "arbitrary")),
        name="inproj",
    )(x, g.reshape(1, D), w)


def _attn_body(slope_ref, *refs, S):
    n_g = len(A_DILATIONS)
    ins, o_ref, scr = refs[:3 * n_g], refs[3 * n_g], refs[3 * n_g + 1:]
    blk = A_STEPS
    slot = pl.program_id(1)
    qi = lax.broadcasted_iota(jnp.int32, (blk, 2 * blk), 0)
    ki = lax.broadcasted_iota(jnp.int32, (blk, 2 * blk), 1)
    delta2 = qi + blk - ki
    valid2 = (delta2 >= 0) & (delta2 <= A_STEPS)
    dist2 = delta2.astype(F32)
    delta1 = delta2[:, blk:]
    valid1 = delta1 >= 0
    dist1 = delta1.astype(F32)
    for g, d in enumerate(A_DILATIONS):
        q_ref, k_ref, v_ref = ins[3 * g:3 * g + 3]
        o_s, m_s, l_s = scr[3 * g], scr[3 * g + 1], scr[3 * g + 2]
        slope = slope_ref[g * A_HPG + slot] * float(d)
        nj = S // (d * blk)
        rows = [pl.ds(r + d * blk * j, blk, stride=d) if d > 1 else pl.ds(blk * j, blk)
                for r in range(d) for j in range(nj)]
        nblk = len(rows)
        take = lambda ref: [ref[rw, :].astype(BF16) for rw in rows]
        q = jnp.stack(take(q_ref), axis=0)
        k_cur, v_cur = take(k_ref), take(v_ref)
        if nj > 1:
            none = jnp.zeros((blk, HEAD_A), BF16)
            prev = lambda t: [none if i % nj == 0 else t[i - 1] for i in range(nblk)]
            k = jnp.stack([jnp.concatenate(pc, axis=0) for pc in zip(prev(k_cur), k_cur)], axis=0)
            v = jnp.stack([jnp.concatenate(pc, axis=0) for pc in zip(prev(v_cur), v_cur)], axis=0)
            first = lax.broadcasted_iota(jnp.int32, (nblk, 1, 1), 0) % nj == 0
            valid = valid2 & ((ki >= blk) | jnp.logical_not(first))
            dist = dist2
        else:
            k, v, valid, dist = jnp.stack(k_cur, axis=0), jnp.stack(v_cur, axis=0), valid1, dist1
        s = _bmm_nt(q, k) * (HEAD_A ** -0.5)
        s = jnp.where(valid, s - slope * dist, NEG_INF)
        m = jnp.max(s, axis=-1, keepdims=True)
        p = jnp.exp(s - m)
        l = jnp.sum(p, axis=-1, keepdims=True)
        o = _bmm(p, v)
        for i, rw in enumerate(rows):
            o_s[rw, :] = o[i]
            m_s[rw, :] = jnp.broadcast_to(m[i], (blk, HEAD_A))
            l_s[rw, :] = jnp.broadcast_to(l[i], (blk, HEAD_A))
    m0, m1, m2 = scr[1][...], scr[4][...], scr[7][...]
    m = jnp.maximum(jnp.maximum(m0, m1), m2)
    e0, e1, e2 = jnp.exp(m0 - m), jnp.exp(m1 - m), jnp.exp(m2 - m)
    den = e0 * scr[2][...] + e1 * scr[5][...] + e2 * scr[8][...]
    o_ref[...] = ((e0 * scr[0][...] + e1 * scr[3][...] + e2 * scr[6][...]) / den).astype(o_ref.dtype)


def _attention(cols, Bn, S):
    T = cols.shape[0]
    slopes = jnp.asarray([2.0 ** (-ALIBI_MAX_BIAS * (h + 1.0) / A_HEADS) for h in range(A_HEADS)], F32)

    def col(off, g):
        return pl.BlockSpec((S, HEAD_A), lambda b, s: (b, off // HEAD_A + g * A_HPG + s))

    in_specs = [pl.BlockSpec(memory_space=pltpu.SMEM)]
    for g in range(len(A_DILATIONS)):
        in_specs += [col(P_AQ, g), col(P_AK, g), col(P_AV, g)]
    return pl.pallas_call(
        functools.partial(_attn_body, S=S),
        grid=(Bn, A_HPG),
        in_specs=in_specs,
        out_specs=pl.BlockSpec((S, HEAD_A), lambda b, s: (b, s)),
        out_shape=jax.ShapeDtypeStruct((T, A_OUT), BF16),
        scratch_shapes=[pltpu.VMEM((S, HEAD_A), F32)] * (3 * len(A_DILATIONS)),
        compiler_params=_cparams(("parallel", "parallel")),
        name="attention",
    )(slopes, *([cols] * (3 * len(A_DILATIONS))))


def _group_masks(ng):
    ri = lax.broadcasted_iota(jnp.int32, (ng, GROUP, GROUP), 1)
    ci = lax.broadcasted_iota(jnp.int32, (ng, GROUP, GROUP), 2)
    same = (ri ^ ci) < CHUNK
    return same, same & (ri >= ci), same & (ri > ci)


def _neumann_inverse_bd(pw, same):
    C = CHUNK
    ng = pw.shape[0]
    spread = lambda x: jnp.where(same, jnp.concatenate([x] * (GROUP // C), axis=1), 0.0)
    ri = lax.broadcasted_iota(jnp.int32, (ng, C, GROUP), 1)
    ci = lax.broadcasted_iota(jnp.int32, (ng, C, GROUP), 2)
    tinv = jnp.where(ri == (ci & (C - 1)), 1.0, 0.0) + pw
    pw = _bmm(pw, spread(pw))
    for _ in range(4):
        both = _bmm(jnp.concatenate([pw, tinv], axis=1), spread(pw))
        pw = both[:, :C]
        tinv = tinv + both[:, C:]
    tinv = tinv + _bmm(tinv, spread(pw))
    return spread(tinv)


def _gdn_body(alog_ref, dtb_ref, q_ref, k_ref, v_ref, z_ref, sm_ref, cw_ref, nw_ref,
              o_ref, S_ref, bq, bk, bv, *, tS):
    C, H, D, G = CHUNK, B_HEADS, B_DIM, GROUP
    n_c = tS // C
    nb = H * n_c
    ng = nb * C // G
    s = pl.program_id(1)

    @pl.when(s == 0)
    def _():
        S_ref[...] = jnp.zeros_like(S_ref)
        for buf in (bq, bk, bv):
            buf[0:8, :] = jnp.zeros((8, B_WIDTH), F32)

    def conv_silu(raw_ref, buf, off):
        x = raw_ref[...]
        buf[8:8 + tS, :] = x
        full = buf[...]
        y = cw_ref[B_CONV - 1:B_CONV, off:off + B_WIDTH] * x
        for j in range(B_CONV - 1):
            y = y + cw_ref[j:j + 1, off:off + B_WIDTH] * pltpu.roll(full, B_CONV - 1 - j, axis=0)[8:8 + tS, :]
        buf[0:8, :] = buf[tS:tS + 8, :]
        return _silu(y)

    def heads(x):
        return jnp.stack([x[:, h * D:(h + 1) * D] for h in range(H)], axis=0).reshape(nb, C, D)

    q = heads(conv_silu(q_ref, bq, 0))
    k = heads(conv_silu(k_ref, bk, B_WIDTH))
    v = heads(conv_silu(v_ref, bv, 2 * B_WIDTH))
    q = q * (lax.rsqrt(jnp.sum(q * q, axis=-1, keepdims=True) + L2_EPS) * (D ** -0.5))
    k = k * lax.rsqrt(jnp.sum(k * k, axis=-1, keepdims=True) + L2_EPS)

    sm = sm_ref[:, 0:LANES]
    lane = lax.broadcasted_iota(jnp.int32, sm.shape, 1)
    lane1 = lax.broadcasted_iota(jnp.int32, (1, LANES), 1)
    dtb = jnp.zeros((1, LANES), F32)
    alog = jnp.zeros((1, LANES), F32)
    for h in range(H):
        dtb = jnp.where(lane1 == SM_ALPHA + h, dtb_ref[h], dtb)
        alog = jnp.where(lane1 == SM_ALPHA + h, alog_ref[h], alog)
    ld_all = -jnp.exp(alog) * _softplus(sm + dtb)
    beta_all = jax.nn.sigmoid(sm)
    ld, beta = [], []
    for h in range(H):
        ld.append(jnp.sum(jnp.where(lane == SM_ALPHA + h, ld_all, 0.0), axis=-1, keepdims=True))
        beta.append(jnp.sum(jnp.where(lane == SM_BETA + h, beta_all, 0.0), axis=-1, keepdims=True))
    ld = jnp.stack(ld, axis=0).reshape(ng, G, 1)
    beta = jnp.stack(beta, axis=0).reshape(ng, G, 1)

    same, tril, strict = _group_masks(ng)
    ri = lax.broadcasted_iota(jnp.int32, (ng, G, G), 1)
    ci = lax.broadcasted_iota(jnp.int32, (ng, G, G), 2)
    grp = lambda t: t.reshape(ng, G, t.shape[-1])
    blk = lambda t: t.reshape(nb, C, t.shape[-1])
    q, k, v = grp(q), grp(k), grp(v)
    LD = jnp.broadcast_to(ld, (ng, G, G))
    g_row = jnp.sum(jnp.where(same & (ri <= ci), LD, 0.0), axis=1, keepdims=True)
    ld_row = jnp.sum(jnp.where(ri == ci, LD, 0.0), axis=1, keepdims=True)
    g_col = jnp.sum(jnp.where(tril, jnp.broadcast_to(ld_row, (ng, G, G)), 0.0), axis=2, keepdims=True)
    g_last = blk(g_col)[:, C - 1:C, :]
    decay = jnp.exp(jnp.where(tril, g_col - g_row, NEG_INF))
    eg = jnp.exp(g_col)
    kb = k * beta
    vb = v * beta
    kq = _bmm_nt(jnp.concatenate([kb, q], axis=1), k)
    m = jnp.where(strict, kq[:, :G] * decay, 0.0)
    att = jnp.where(tril, kq[:, G:] * decay, 0.0)
    m_cat = m[:, 0:C]
    for i in range(1, G // C):
        m_cat = m_cat + m[:, i * C:(i + 1) * C]
    tinv = _neumann_inverse_bd(-m_cat, same)
    wu = _bmm(tinv, jnp.concatenate([kb * eg, vb], axis=2))
    k_dec = blk(k) * jnp.exp(g_last - blk(g_col))
    gz = _bmm_tn(k_dec, blk(wu))
    aw = _bmm(att, wu)
    q_eff = blk(q * eg - aw[:, :, :D])
    o0 = blk(aw[:, :, D:])
    cd = jnp.exp(g_last)
    S = [S_ref[h] for h in range(H)]
    outs = [[None] * n_c for _ in range(H)]
    for c in range(n_c):
        for h in range(H):
            i = h * n_c + c
            outs[h][c] = _mm(q_eff[i], S[h]) + o0[i]
            S[h] = S[h] * cd[i] - _mm(gz[i, :, :D], S[h]) + gz[i, :, D:]
    nw = nw_ref[...]
    for h in range(H):
        S_ref[h] = S[h]
        o = jnp.concatenate(outs[h], axis=0)
        hs = slice(h * D, (h + 1) * D)
        o_ref[:, hs] = (_rms_rows(o, nw) * _silu(z_ref[:, hs])).astype(o_ref.dtype)


def _gdn(cols, b_conv, a_log, dt_bias, norm_w, Bn, S, tS=256):
    T = cols.shape[0]
    nS = S // tS
    W = B_WIDTH

    def col(off):
        return pl.BlockSpec((tS, W), lambda b, s: (b * nS + s, off // W))

    smem = pl.BlockSpec(memory_space=pltpu.SMEM)
    return pl.pallas_call(
        functools.partial(_gdn_body, tS=tS),
        grid=(Bn, nS),
        in_specs=[smem, smem, col(P_BQ), col(P_BK), col(P_BV), col(P_BZ),
                  pl.BlockSpec((tS, SM_W), lambda b, s: (b * nS + s, P_SM // SM_W)),
                  pl.BlockSpec((B_CONV, 3 * W), lambda b, s: (0, 0)),
                  pl.BlockSpec((1, B_DIM), lambda b, s: (0, 0))],
        out_specs=pl.BlockSpec((tS, W), lambda b, s: (b * nS + s, 0)),
        out_shape=jax.ShapeDtypeStruct((T, W), BF16),
        scratch_shapes=[pltpu.VMEM((B_HEADS, B_DIM, B_DIM), F32)] + [pltpu.VMEM((tS + 8, W), F32)] * 3,
        compiler_params=_cparams(("parallel", "arbitrary")),
        name="gdn",
    )(a_log, dt_bias, cols, cols, cols, cols, cols, b_conv, norm_w.reshape(1, B_DIM))


def _rwkv_body(r_ref, k_ref, v_ref, sm_ref, mur_ref, muk_ref, muv_ref, musm_ref,
               w0_ref, w2_ref, a0_ref, a2_ref, g2_ref, kkw_ref, kaw_ref, rk_ref, gnw_ref, gnb_ref,
               o_ref, S_ref, br, bk, bv, bsm, *, tS):
    C = CHUNK
    P = 2 * C_DIM
    NP = C_WIDTH // P
    s = pl.program_id(1)

    @pl.when(s == 0)
    def _():
        S_ref[...] = jnp.zeros_like(S_ref)
        for buf in (br, bk, bv, bsm):
            buf[0:8, :] = jnp.zeros((8, buf.shape[1]), F32)

    def shift(x_ref, buf, mu_ref):
        x = x_ref[...]
        buf[8:8 + tS, :] = x
        prev = pltpu.roll(buf[...], 1, axis=0)[8:8 + tS, :]
        buf[0:8, :] = buf[tS:tS + 8, :]
        return x + (prev - x) * mu_ref[...]

    r = shift(r_ref, br, mur_ref)
    k = shift(k_ref, bk, muk_ref)
    v = shift(v_ref, bv, muv_ref)
    sm = shift(sm_ref, bsm, musm_ref)
    xw, xa, xg = sm[:, 0:LANES], sm[:, SM_XA:SM_XA + LANES], sm[:, SM_XG:SM_XG + C_GATE_LORA]
    lw = -C_DECAY_SCALE * jax.nn.sigmoid(w0_ref[...] + _mm(jnp.tanh(xw), w2_ref[...]))
    a = jax.nn.sigmoid(a0_ref[...] + _mm(xa, a2_ref[...]))
    gate = _mm(jax.nn.sigmoid(xg), g2_ref[...])
    kkraw = k * kkw_ref[...]
    k = k * (1.0 + (a - 1.0) * kaw_ref[...])

    n_c = tS // C
    nb = NP * n_c
    ng = nb * 2 * C // GROUP
    m0 = lax.broadcasted_iota(jnp.int32, (1, P), 1) < C_DIM
    rs = lax.broadcasted_iota(jnp.int32, (tS, tS), 0)
    cs = lax.broadcasted_iota(jnp.int32, (tS, tS), 1)
    tril_b = jnp.where((rs >= cs) & ((rs ^ cs) < C), 1.0, 0.0).astype(BF16)
    r2 = lax.broadcasted_iota(jnp.int32, (1, 2 * C, 2 * C), 1)
    c2 = lax.broadcasted_iota(jnp.int32, (1, 2 * C, 2 * C), 2)
    strict2 = r2 > c2
    rt = lax.broadcasted_iota(jnp.int32, (1, C, 2 * C), 1)
    ct = lax.broadcasted_iota(jnp.int32, (1, C, 2 * C), 2)
    incl = rt >= (ct & (C - 1))

    def seg_sum(x):
        s0 = jnp.sum(jnp.where(m0, x, 0.0), axis=-1, keepdims=True)
        s1 = jnp.sum(jnp.where(m0, 0.0, x), axis=-1, keepdims=True)
        return jnp.where(m0, s0, s1)

    def pairs(x):
        return jnp.stack([x[:, p * P:(p + 1) * P] for p in range(NP)], axis=0).reshape(nb, C, P)

    def stack(x):
        return jnp.concatenate([jnp.where(m0, x, 0.0), jnp.where(m0, 0.0, x)], axis=1)

    lw_hi = lw.astype(BF16)
    lw_lo = (lw - lw_hi.astype(F32)).astype(BF16)
    cum2 = jnp.dot(tril_b, jnp.concatenate([lw_hi, lw_lo], axis=1), preferred_element_type=F32)
    cum = cum2[:, :C_WIDTH] + cum2[:, C_WIDTH:]
    kkraw = pairs(kkraw)
    kk = kkraw * lax.rsqrt(seg_sum(kkraw * kkraw) + L2_EPS)
    bv = kk * pairs(a)
    w_inc = pairs(jnp.exp(cum))
    w_inv = pairs(jnp.exp(-cum))
    w_end = w_inc[:, C - 1:C, :]
    r_t = pairs(r) * w_inc
    a_st = stack(-kk * pairs(jnp.exp(cum - lw)))
    b_st = stack(bv * w_inv)
    k_st = stack(pairs(k) * w_inv)
    v_st = stack(pairs(v))
    aa = _bmm_nt(jnp.concatenate([a_st, r_t], axis=1), jnp.concatenate([b_st, k_st], axis=1))
    a_ab = jnp.where(strict2, aa[:, :2 * C, :2 * C], 0.0)
    a_ak = jnp.where(strict2, aa[:, :2 * C, 2 * C:], 0.0)
    a_rb = jnp.where(incl, aa[:, 2 * C:, :2 * C], 0.0)
    a_rk = jnp.where(incl, aa[:, 2 * C:, 2 * C:], 0.0)
    a4 = a_ab.reshape(ng, 2, 2 * C, 2 * C)
    a_cat = jnp.concatenate([a4[:, 0, :C] + a4[:, 0, C:], a4[:, 1, :C] + a4[:, 1, C:]], axis=2)
    tinv = _neumann_inverse_bd(a_cat, _group_masks(ng)[0])
    x = jnp.concatenate([a_st, _bmm(a_ak, v_st)], axis=2)
    wu = _bmm(tinv, x.reshape(ng, GROUP, 2 * P)).reshape(nb, 2 * C, 2 * P)
    rw = _bmm(a_rb, wu)
    r_eff = r_t + rw[:, :, :P]
    o0 = rw[:, :, P:] + _bmm(a_rk, v_st)
    mz = _bmm_tn(wu, b_st * w_end)
    z = mz[:, P:, :] + _bmm_tn(v_st, k_st * w_end)
    S = [S_ref[p] for p in range(NP)]
    outs = [[None] * n_c for _ in range(NP)]
    for c in range(n_c):
        for p in range(NP):
            i = p * n_c + c
            outs[p][c] = _mm_nt(r_eff[i], S[p]) + o0[i]
            S[p] = S[p] * w_end[i] + _mm(S[p], mz[i, :P, :]) + z[i]
    for p in range(NP):
        S_ref[p] = S[p]
        ps = slice(p * P, (p + 1) * P)
        o = jnp.concatenate(outs[p], axis=0)
        mean = seg_sum(o) * (1.0 / C_DIM)
        d = o - mean
        var = seg_sum(d * d) * (1.0 / C_DIM)
        on = d * lax.rsqrt(var + C_GN_EPS) * gnw_ref[:, ps] + gnb_ref[:, ps]
        bonus = seg_sum(r[:, ps] * k[:, ps] * rk_ref[:, ps]) * v[:, ps]
        o_ref[:, ps] = ((on + bonus) * gate[:, ps]).astype(o_ref.dtype)


def _rwkv(cols, mu_r, mu_k, mu_v, mu_sm, w0, w2p, a0, a2p, g2, k_k, k_a, r_k, gn_w, gn_b, Bn, S, tS=256):
    T = cols.shape[0]
    nS = S // tS
    W = C_WIDTH
    P = 2 * C_DIM

    def col(off, width):
        return pl.BlockSpec((tS, width), lambda b, s: (b * nS + s, off // width))

    def par(rows, width=W):
        return pl.BlockSpec((rows, width), lambda b, s: (0, 0))

    return pl.pallas_call(
        functools.partial(_rwkv_body, tS=tS),
        grid=(Bn, nS),
        in_specs=[col(P_CR, W), col(P_CK, W), col(P_CV, W), col(P_SM, SM_W), par(1), par(1), par(1),
                  par(1, SM_W), par(1), par(LANES), par(1), par(LANES), par(C_GATE_LORA), par(1), par(1),
                  par(1), par(1), par(1)],
        out_specs=pl.BlockSpec((tS, W), lambda b, s: (b * nS + s, 0)),
        out_shape=jax.ShapeDtypeStruct((T, W), BF16),
        scratch_shapes=[pltpu.VMEM((W // P, P, P), F32)] + [pltpu.VMEM((tS + 8, W), F32)] * 3
                       + [pltpu.VMEM((tS + 8, SM_W), F32)],
        compiler_params=_cparams(("parallel", "arbitrary")),
        name="rwkv",
    )(cols, cols, cols, cols, mu_r, mu_k, mu_v, mu_sm, w0, w2p, a0, a2p, g2, k_k, k_a, r_k, gn_w, gn_b)


def _merge_body(x_ref, ya_ref, yb_ref, yc_ref, g_ref, pa_ref, pb_ref, pc_ref, wo_ref, o_ref):
    D = x_ref.shape[1]
    merged = (jax.nn.sigmoid(g_ref[:, 0:D]) * jnp.dot(ya_ref[...], pa_ref[...], preferred_element_type=F32)
              + jax.nn.sigmoid(g_ref[:, D:2 * D]) * jnp.dot(yb_ref[...], pb_ref[...], preferred_element_type=F32)
              + jax.nn.sigmoid(g_ref[:, 2 * D:]) * jnp.dot(yc_ref[...], pc_ref[...], preferred_element_type=F32))
    o_ref[...] = x_ref[...] + jnp.dot(merged.astype(BF16), wo_ref[...], preferred_element_type=F32)


def _merge(x, y_a, y_b, y_c, cols, proj_a, proj_b, proj_c, w_out, l, tm=256):
    T, D = x.shape

    def rows(width):
        return pl.BlockSpec((tm, width), lambda i: (i, 0))

    def resident(kdim):
        return pl.BlockSpec((None, kdim, D), lambda i: (l, 0, 0), pipeline_mode=pl.Buffered(1))

    return pl.pallas_call(
        _merge_body,
        grid=(T // tm,),
        in_specs=[rows(D), rows(A_OUT), rows(B_WIDTH), rows(C_WIDTH),
                  pl.BlockSpec((tm, 3 * D), lambda i: (i, P_G // (3 * D))),
                  resident(A_OUT), resident(B_WIDTH), resident(C_WIDTH), resident(D)],
        out_specs=rows(D),
        out_shape=jax.ShapeDtypeStruct((T, D), F32),
        compiler_params=_cparams(("parallel",)),
        name="merge",
    )(x, y_a, y_b, y_c, cols, proj_a, proj_b, proj_c, w_out)


def _cast_body(w_ref, o_ref):
    o_ref[...] = w_ref[...].astype(BF16)


def _to_bf16(w):
    L, R, Cw = w.shape
    tr = 128 if Cw > 4096 else 256
    spec = pl.BlockSpec((None, tr, Cw), lambda l, i: (l, i, 0))
    return pl.pallas_call(
        _cast_body,
        grid=(L, R // tr),
        in_specs=[spec],
        out_specs=spec,
        out_shape=jax.ShapeDtypeStruct(w.shape, BF16),
        compiler_params=_cparams(("parallel", "parallel")),
        name="cast_bf16",
    )(w)


def _pack_body(w_ref, sm_ref, o_ref, *, n_big):
    t = pl.program_id(1)
    o_ref[...] = jnp.where(t >= n_big, sm_ref[...], w_ref[0]).astype(BF16)


def _pack_w_in(w, tr=512):
    L, D, _ = w.shape
    a_cols = 3 * A_HEADS * HEAD_A
    b_ab = a_cols + 4 * B_WIDTH
    c0 = b_ab + 2 * B_HEADS
    c_xw = c0 + 3 * C_WIDTH
    c_xa = c_xw + C_DECAY_LORA
    c_xg = c_xa + C_ICLR_LORA
    g0 = c_xg + C_GATE_LORA
    wt = jnp.swapaxes(w, 1, 2)
    z = lambda n: jnp.zeros((L, n, D), w.dtype)
    small = jnp.concatenate([wt[:, c_xw:c_xa], wt[:, b_ab:c0], z(16), wt[:, c_xa:c_xg], z(32), wt[:, c_xg:g0]],
                            axis=1)
    n_big = P_SM // tr
    n_small = SM_W // tr

    def src_row(t):
        r = t * tr
        off = jnp.where(r < P_BQ, g0 - P_G,
                        jnp.where(r < P_CR, a_cols - P_BQ, jnp.where(r < P_AQ, c0 - P_CR, 0 - P_AQ)))
        return pl.multiple_of(jnp.where(t < n_big, r + off, 0), 16)

    return pl.pallas_call(
        functools.partial(_pack_body, n_big=n_big),
        grid=(L, n_big + n_small),
        in_specs=[pl.BlockSpec((pl.Element(1), pl.Element(tr), pl.Element(D)), lambda l, t: (l, src_row(t), 0)),
                  pl.BlockSpec((None, tr, D), lambda l, t: (l, jnp.clip(t - n_big, 0, n_small - 1), 0))],
        out_specs=pl.BlockSpec((None, tr, D), lambda l, t: (l, t, 0)),
        out_shape=jax.ShapeDtypeStruct((L, P_COLS, D), BF16),
        compiler_params=_cparams(("parallel", "arbitrary")),
        name="pack_w_in",
    )(wt, small)


def _pad_rows(w, n):
    return jnp.concatenate([w, jnp.zeros((n - w.shape[0], w.shape[1]), w.dtype)], axis=0)


def _mixers(x, cols, l, Bn, S, b_conv, b_a_log, b_dt_bias, b_norm, c_mu, c_w0, c_w2, c_a0, c_a2, c_g2,
            c_k_k, c_k_a, c_r_k, c_gn_w, c_gn_b):
    y_a = _attention(cols, Bn, S)
    y_b = _gdn(cols, b_conv[l], b_a_log[l], b_dt_bias[l], b_norm[l], Bn, S)
    W = C_WIDTH
    mu = c_mu[l]
    z = lambda n: jnp.zeros((n,), F32)
    mu_sm = jnp.concatenate([mu[3 * W:3 * W + C_DECAY_LORA], z(32),
                             mu[3 * W + C_DECAY_LORA:3 * W + C_DECAY_LORA + C_ICLR_LORA], z(32),
                             mu[3 * W + C_DECAY_LORA + C_ICLR_LORA:]])
    row = lambda t: t.reshape(1, -1)
    y_c = _rwkv(cols, row(mu[0:W]), row(mu[W:2 * W]), row(mu[2 * W:3 * W]), row(mu_sm),
                row(c_w0[l]), _pad_rows(c_w2[l], LANES).astype(BF16), row(c_a0[l]),
                _pad_rows(c_a2[l], LANES).astype(BF16), c_g2[l].astype(BF16),
                row(c_k_k[l]), row(c_k_a[l]), row(c_r_k[l]), row(c_gn_w[l]), row(c_gn_b[l]), Bn, S)
    return y_a, y_b, y_c


def kernel(x, ffn1_norm, ffn1_w_gu, ffn1_w_down, mix_norm, w_in, b_conv, b_a_log, b_dt_bias, b_norm,
           c_mu, c_w0, c_w2, c_a0, c_a2, c_g2, c_k_k, c_k_a, c_r_k, c_gn_w, c_gn_b,
           proj_a, proj_b, proj_c, w_out, ffn2_norm, ffn2_w_gu, ffn2_w_down, final_norm):
    Bn, S, D = x.shape
    depth = w_in.shape[0]
    x = x.reshape(Bn * S, D)
    f1_gu, f1_dn, f2_gu, f2_dn = (_to_bf16(w) for w in (ffn1_w_gu, ffn1_w_down, ffn2_w_gu, ffn2_w_down))
    p_a, p_b, p_c, w_o = (_to_bf16(w) for w in (proj_a, proj_b, proj_c, w_out))
    w_packed = _pack_w_in(w_in)
    for l in range(depth):
        x = _ffn(x, ffn1_norm[l], f1_gu, f1_dn, l)
        cols = _inproj(x, mix_norm[l], w_packed, l)
        y_a, y_b, y_c = _mixers(x, cols, l, Bn, S, b_conv, b_a_log, b_dt_bias, b_norm, c_mu, c_w0,
                                c_w2, c_a0, c_a2, c_g2, c_k_k, c_k_a, c_r_k, c_gn_w, c_gn_b)
        x = _merge(x, y_a, y_b, y_c, cols, p_a, p_b, p_c, w_o, l)
        x = _ffn(x, ffn2_norm[l], f2_gu, f2_dn, l, out_norm=final_norm if l == depth - 1 else None)
    return x.reshape(Bn, S, D)
```

```python
import functools

import jax
import jax.numpy as jnp
from jax import lax
from jax.experimental import pallas as pl
from jax.experimental.pallas import tpu as pltpu

F32 = jnp.float32
BF16 = jnp.bfloat16

D_MODEL = 2048
D_FF = 5632
A_DILATIONS = (1, 4, 16)
A_HPG = 4
A_HEADS = 12
HEAD_A = 128
A_OUT = A_HPG * HEAD_A
A_STEPS = 128
ALIBI_MAX_BIAS = 8.0
B_HEADS = 8
B_DIM = 128
B_WIDTH = B_HEADS * B_DIM
B_CONV = 4
C_DIM = 64
C_HEADS = 16
C_WIDTH = C_HEADS * C_DIM
C_DECAY_LORA = 96
C_ICLR_LORA = 96
C_GATE_LORA = 256
C_GN_EPS = 64e-5
C_DECAY_SCALE = 0.6065306597126334
NORM_EPS = 1e-6
L2_EPS = 1e-6
NEG_INF = -1e30
LANES = 128
CHUNK = 64
GROUP = 4 * CHUNK
FFN_TF = 512

P_G = 0
P_BQ, P_BK, P_BV, P_BZ = 6144, 7168, 8192, 9216
P_CR, P_CK, P_CV = 10240, 11264, 12288
P_AQ, P_AK, P_AV = 13312, 14848, 16384
P_SM = 17920
P_COLS = 18432
SM_W = 512
SM_ALPHA, SM_BETA, SM_XA, SM_XG = 96, 104, 128, 256

VMEM_LIMIT = 56 * 1024 * 1024


def _cparams(sem):
    return pltpu.CompilerParams(dimension_semantics=sem, vmem_limit_bytes=VMEM_LIMIT)


def _mm(a, b):
    return jnp.dot(a.astype(BF16), b.astype(BF16), preferred_element_type=F32)


def _mm_nt(a, b):
    return lax.dot_general(a.astype(BF16), b.astype(BF16), (((1,), (1,)), ((), ())),
                           preferred_element_type=F32)


def _mm_tn(a, b):
    return lax.dot_general(a.astype(BF16), b.astype(BF16), (((0,), (0,)), ((), ())),
                           preferred_element_type=F32)


def _bmm(a, b):
    return jnp.einsum("bmk,bkn->bmn", a.astype(BF16), b.astype(BF16), preferred_element_type=F32)


def _bmm_nt(a, b):
    return jnp.einsum("bmk,bnk->bmn", a.astype(BF16), b.astype(BF16), preferred_element_type=F32)


def _bmm_tn(a, b):
    return jnp.einsum("bkm,bkn->bmn", a.astype(BF16), b.astype(BF16), preferred_element_type=F32)


def _silu(x):
    return x * jax.nn.sigmoid(x)


def _softplus(x):
    return jnp.maximum(x, 0.0) + jnp.log1p(jnp.exp(-jnp.abs(x)))


def _rms_rows(x, g):
    ms = jnp.mean(x * x, axis=-1, keepdims=True)
    return x * lax.rsqrt(ms + NORM_EPS) * g


def _ffn_body(x_ref, g_ref, wgu_ref, wd_ref, *rest, with_out_norm):
    o_ref, h_ref = rest[-2], rest[-1]

    @pl.when(pl.program_id(1) == 0)
    def _():
        x = x_ref[...]
        h_ref[...] = _rms_rows(x, g_ref[...]).astype(BF16)
        o_ref[...] = x

    tf = wd_ref.shape[0]
    gu = jnp.dot(h_ref[...], wgu_ref[...], preferred_element_type=F32)
    act = (0.5 * _silu(gu[:, :tf]) * gu[:, tf:]).astype(BF16)
    o_ref[...] += jnp.dot(act, wd_ref[...], preferred_element_type=F32)

    if with_out_norm:
        @pl.when(pl.program_id(1) == pl.num_programs(1) - 1)
        def _():
            o_ref[...] = _rms_rows(o_ref[...], rest[0][...])


def _ffn(x, g, w_gu, w_down, l, out_norm=None, tm=1024, tf=FFN_TF):
    T, D = x.shape
    dff = w_down.shape[1]
    nj = dff // tf
    vec = pl.BlockSpec((1, D), lambda i, j: (0, 0))
    extra = [] if out_norm is None else [out_norm.reshape(1, D)]
    return pl.pallas_call(
        functools.partial(_ffn_body, with_out_norm=out_norm is not None),
        grid=(T // tm, nj),
        in_specs=[
            pl.BlockSpec((tm, D), lambda i, j: (i, 0)),
            vec,
            pl.BlockSpec((None, D, 2 * tf), lambda i, j: (l, 0, j)),
            pl.BlockSpec((None, tf, D), lambda i, j: (l, j, 0)),
        ] + [vec] * len(extra),
        out_specs=pl.BlockSpec((tm, D), lambda i, j: (i, 0)),
        out_shape=jax.ShapeDtypeStruct((T, D), F32),
        scratch_shapes=[pltpu.VMEM((tm, D), BF16)],
        compiler_params=_cparams(("parallel", "arbitrary")),
        name="ffn",
    )(x, g.reshape(1, D), w_gu, w_down, *extra)


def _inproj_body(x_ref, g_ref, w_ref, o_ref, h_ref):
    @pl.when(pl.program_id(1) == 0)
    def _():
        h_ref[...] = _rms_rows(x_ref[...], g_ref[...]).astype(BF16)

    o_ref[...] = _mm_nt(h_ref[...], w_ref[...])


def _inproj(x, g, w, l, tm=1024, tn=2048):
    T, D = x.shape
    N = w.shape[1]
    return pl.pallas_call(
        _inproj_body,
        grid=(T // tm, N // tn),
        in_specs=[
            pl.BlockSpec((tm, D), lambda i, j: (i, 0)),
            pl.BlockSpec((1, D), lambda i, j: (0, 0)),
            pl.BlockSpec((None, tn, D), lambda i, j: (l, j, 0)),
        ],
        out_specs=pl.BlockSpec((tm, tn), lambda i, j: (i, j)),
        out_shape=jax.ShapeDtypeStruct((T, N), F32),
        scratch_shapes=[pltpu.VMEM((tm, D), BF16)],
        compiler_params=_cparams(("parallel", "arbitrary")),
        name="inproj",
    )(x, g.reshape(1, D), w)


def _attn_body(slope_ref, *refs, S):
    n_g = len(A_DILATIONS)
    ins, o_ref, scr = refs[:3 * n_g], refs[3 * n_g], refs[3 * n_g + 1:]
    blk = A_STEPS
    slot = pl.program_id(1)
    qi = lax.broadcasted_iota(jnp.int32, (blk, 2 * blk), 0)
    ki = lax.broadcasted_iota(jnp.int32, (blk, 2 * blk), 1)
    delta2 = qi + blk - ki
    valid2 = (delta2 >= 0) & (delta2 <= A_STEPS)
    dist2 = delta2.astype(F32)
    delta1 = delta2[:, blk:]
    valid1 = delta1 >= 0
    dist1 = delta1.astype(F32)
    for g, d in enumerate(A_DILATIONS):
        q_ref, k_ref, v_ref = ins[3 * g:3 * g + 3]
        o_s, m_s, l_s = scr[3 * g], scr[3 * g + 1], scr[3 * g + 2]
        slope = slope_ref[g * A_HPG + slot] * float(d)
        nj = S // (d * blk)
        rows = [pl.ds(r + d * blk * j, blk, stride=d) if d > 1 else pl.ds(blk * j, blk)
                for r in range(d) for j in range(nj)]
        nblk = len(rows)
        take = lambda ref: [ref[rw, :].astype(BF16) for rw in rows]
        q = jnp.stack(take(q_ref), axis=0)
        k_cur, v_cur = take(k_ref), take(v_ref)
        if nj > 1:
            none = jnp.zeros((blk, HEAD_A), BF16)
            prev = lambda t: [none if i % nj == 0 else t[i - 1] for i in range(nblk)]
            k = jnp.stack([jnp.concatenate(pc, axis=0) for pc in zip(prev(k_cur), k_cur)], axis=0)
            v = jnp.stack([jnp.concatenate(pc, axis=0) for pc in zip(prev(v_cur), v_cur)], axis=0)
            first = lax.broadcasted_iota(jnp.int32, (nblk, 1, 1), 0) % nj == 0
            valid = valid2 & ((ki >= blk) | jnp.logical_not(first))
            dist = dist2
        else:
            k, v, valid, dist = jnp.stack(k_cur, axis=0), jnp.stack(v_cur, axis=0), valid1, dist1
        s = _bmm_nt(q, k) * (HEAD_A ** -0.5)
        s = jnp.where(valid, s - slope * dist, NEG_INF)
        m = jnp.max(s, axis=-1, keepdims=True)
        p = jnp.exp(s - m)
        l = jnp.sum(p, axis=-1, keepdims=True)
        o = _bmm(p, v)
        for i, rw in enumerate(rows):
            o_s[rw, :] = o[i]
            m_s[rw, :] = jnp.broadcast_to(m[i], (blk, HEAD_A))
            l_s[rw, :] = jnp.broadcast_to(l[i], (blk, HEAD_A))
    m0, m1, m2 = scr[1][...], scr[4][...], scr[7][...]
    m = jnp.maximum(jnp.maximum(m0, m1), m2)
    e0, e1, e2 = jnp.exp(m0 - m), jnp.exp(m1 - m), jnp.exp(m2 - m)
    den = e0 * scr[2][...] + e1 * scr[5][...] + e2 * scr[8][...]
    o_ref[...] = ((e0 * scr[0][...] + e1 * scr[3][...] + e2 * scr[6][...]) / den).astype(o_ref.dtype)


def _attention(cols, Bn, S):
    T = cols.shape[0]
    slopes = jnp.asarray([2.0 ** (-ALIBI_MAX_BIAS * (h + 1.0) / A_HEADS) for h in range(A_HEADS)], F32)

    def col(off, g):
        return pl.BlockSpec((S, HEAD_A), lambda b, s: (b, off // HEAD_A + g * A_HPG + s))

    in_specs = [pl.BlockSpec(memory_space=pltpu.SMEM)]
    for g in range(len(A_DILATIONS)):
        in_specs += [col(P_AQ, g), col(P_AK, g), col(P_AV, g)]
    return pl.pallas_call(
        functools.partial(_attn_body, S=S),
        grid=(Bn, A_HPG),
        in_specs=in_specs,
        out_specs=pl.BlockSpec((S, HEAD_A), lambda b, s: (b, s)),
        out_shape=jax.ShapeDtypeStruct((T, A_OUT), BF16),
        scratch_shapes=[pltpu.VMEM((S, HEAD_A), F32)] * (3 * len(A_DILATIONS)),
        compiler_params=_cparams(("parallel", "parallel")),
        name="attention",
    )(slopes, *([cols] * (3 * len(A_DILATIONS))))


def _group_masks(ng):
    ri = lax.broadcasted_iota(jnp.int32, (ng, GROUP, GROUP), 1)
    ci = lax.broadcasted_iota(jnp.int32, (ng, GROUP, GROUP), 2)
    same = (ri ^ ci) < CHUNK
    return same, same & (ri >= ci), same & (ri > ci)


def _neumann_inverse_bd(pw, same):
    C = CHUNK
    ng = pw.shape[0]
    spread = lambda x: jnp.where(same, jnp.concatenate([x] * (GROUP // C), axis=1), 0.0)
    ri = lax.broadcasted_iota(jnp.int32, (ng, C, GROUP), 1)
    ci = lax.broadcasted_iota(jnp.int32, (ng, C, GROUP), 2)
    tinv = jnp.where(ri == (ci & (C - 1)), 1.0, 0.0) + pw
    pw = _bmm(pw, spread(pw))
    for _ in range(4):
        both = _bmm(jnp.concatenate([pw, tinv], axis=1), spread(pw))
        pw = both[:, :C]
        tinv = tinv + both[:, C:]
    tinv = tinv + _bmm(tinv, spread(pw))
    return spread(tinv)


def _gdn_body(alog_ref, dtb_ref, q_ref, k_ref, v_ref, z_ref, sm_ref, cw_ref, nw_ref,
              o_ref, S_ref, bq, bk, bv, *, tS):
    C, H, D, G = CHUNK, B_HEADS, B_DIM, GROUP
    n_c = tS // C
    nb = H * n_c
    ng = nb * C // G
    s = pl.program_id(1)

    @pl.when(s == 0)
    def _():
        S_ref[...] = jnp.zeros_like(S_ref)
        for buf in (bq, bk, bv):
            buf[0:8, :] = jnp.zeros((8, B_WIDTH), F32)

    def conv_silu(raw_ref, buf, off):
        x = raw_ref[...]
        buf[8:8 + tS, :] = x
        full = buf[...]
        y = cw_ref[B_CONV - 1:B_CONV, off:off + B_WIDTH] * x
        for j in range(B_CONV - 1):
            y = y + cw_ref[j:j + 1, off:off + B_WIDTH] * pltpu.roll(full, B_CONV - 1 - j, axis=0)[8:8 + tS, :]
        buf[0:8, :] = buf[tS:tS + 8, :]
        return _silu(y)

    def heads(x):
        return jnp.stack([x[:, h * D:(h + 1) * D] for h in range(H)], axis=0).reshape(nb, C, D)

    q = heads(conv_silu(q_ref, bq, 0))
    k = heads(conv_silu(k_ref, bk, B_WIDTH))
    v = heads(conv_silu(v_ref, bv, 2 * B_WIDTH))
    q = q * (lax.rsqrt(jnp.sum(q * q, axis=-1, keepdims=True) + L2_EPS) * (D ** -0.5))
    k = k * lax.rsqrt(jnp.sum(k * k, axis=-1, keepdims=True) + L2_EPS)

    sm = sm_ref[:, 0:LANES]
    lane = lax.broadcasted_iota(jnp.int32, sm.shape, 1)
    lane1 = lax.broadcasted_iota(jnp.int32, (1, LANES), 1)
    dtb = jnp.zeros((1, LANES), F32)
    alog = jnp.zeros((1, LANES), F32)
    for h in range(H):
        dtb = jnp.where(lane1 == SM_ALPHA + h, dtb_ref[h], dtb)
        alog = jnp.where(lane1 == SM_ALPHA + h, alog_ref[h], alog)
    ld_all = -jnp.exp(alog) * _softplus(sm + dtb)
    beta_all = jax.nn.sigmoid(sm)
    ld, beta = [], []
    for h in range(H):
        ld.append(jnp.sum(jnp.where(lane == SM_ALPHA + h, ld_all, 0.0), axis=-1, keepdims=True))
        beta.append(jnp.sum(jnp.where(lane == SM_BETA + h, beta_all, 0.0), axis=-1, keepdims=True))
    ld = jnp.stack(ld, axis=0).reshape(ng, G, 1)
    beta = jnp.stack(beta, axis=0).reshape(ng, G, 1)

    same, tril, strict = _group_masks(ng)
    ri = lax.broadcasted_iota(jnp.int32, (ng, G, G), 1)
    ci = lax.broadcasted_iota(jnp.int32, (ng, G, G), 2)
    grp = lambda t: t.reshape(ng, G, t.shape[-1])
    blk = lambda t: t.reshape(nb, C, t.shape[-1])
    q, k, v = grp(q), grp(k), grp(v)
    LD = jnp.broadcast_to(ld, (ng, G, G))
    g_row = jnp.sum(jnp.where(same & (ri <= ci), LD, 0.0), axis=1, keepdims=True)
    ld_row = jnp.sum(jnp.where(ri == ci, LD, 0.0), axis=1, keepdims=True)
    g_col = jnp.sum(jnp.where(tril, jnp.broadcast_to(ld_row, (ng, G, G)), 0.0), axis=2, keepdims=True)
    g_last = blk(g_col)[:, C - 1:C, :]
    decay = jnp.exp(jnp.where(tril, g_col - g_row, NEG_INF))
    eg = jnp.exp(g_col)
    kb = k * beta
    vb = v * beta
    kq = _bmm_nt(jnp.concatenate([kb, q], axis=1), k)
    m = jnp.where(strict, kq[:, :G] * decay, 0.0)
    att = jnp.where(tril, kq[:, G:] * decay, 0.0)
    m_cat = m[:, 0:C]
    for i in range(1, G // C):
        m_cat = m_cat + m[:, i * C:(i + 1) * C]
    tinv = _neumann_inverse_bd(-m_cat, same)
    wu = _bmm(tinv, jnp.concatenate([kb * eg, vb], axis=2))
    k_dec = blk(k) * jnp.exp(g_last - blk(g_col))
    gz = _bmm_tn(k_dec, blk(wu))
    aw = _bmm(att, wu)
    q_eff = blk(q * eg - aw[:, :, :D])
    o0 = blk(aw[:, :, D:])
    cd = jnp.exp(g_last)
    S = [S_ref[h] for h in range(H)]
    outs = [[None] * n_c for _ in range(H)]
    for c in range(n_c):
        for h in range(H):
            i = h * n_c + c
            outs[h][c] = _mm(q_eff[i], S[h]) + o0[i]
            S[h] = S[h] * cd[i] - _mm(gz[i, :, :D], S[h]) + gz[i, :, D:]
    nw = nw_ref[...]
    for h in range(H):
        S_ref[h] = S[h]
        o = jnp.concatenate(outs[h], axis=0)
        hs = slice(h * D, (h + 1) * D)
        o_ref[:, hs] = (_rms_rows(o, nw) * _silu(z_ref[:, hs])).astype(o_ref.dtype)


def _gdn(cols, b_conv, a_log, dt_bias, norm_w, Bn, S, tS=256):
    T = cols.shape[0]
    nS = S // tS
    W = B_WIDTH

    def col(off):
        return pl.BlockSpec((tS, W), lambda b, s: (b * nS + s, off // W))

    smem = pl.BlockSpec(memory_space=pltpu.SMEM)
    return pl.pallas_call(
        functools.partial(_gdn_body, tS=tS),
        grid=(Bn, nS),
        in_specs=[smem, smem, col(P_BQ), col(P_BK), col(P_BV), col(P_BZ),
                  pl.BlockSpec((tS, SM_W), lambda b, s: (b * nS + s, P_SM // SM_W)),
                  pl.BlockSpec((B_CONV, 3 * W), lambda b, s: (0, 0)),
                  pl.BlockSpec((1, B_DIM), lambda b, s: (0, 0))],
        out_specs=pl.BlockSpec((tS, W), lambda b, s: (b * nS + s, 0)),
        out_shape=jax.ShapeDtypeStruct((T, W), BF16),
        scratch_shapes=[pltpu.VMEM((B_HEADS, B_DIM, B_DIM), F32)] + [pltpu.VMEM((tS + 8, W), F32)] * 3,
        compiler_params=_cparams(("parallel", "arbitrary")),
        name="gdn",
    )(a_log, dt_bias, cols, cols, cols, cols, cols, b_conv, norm_w.reshape(1, B_DIM))


def _rwkv_body(r_ref, k_ref, v_ref, sm_ref, mur_ref, muk_ref, muv_ref, musm_ref,
               w0_ref, w2_ref, a0_ref, a2_ref, g2_ref, kkw_ref, kaw_ref, rk_ref, gnw_ref, gnb_ref,
               o_ref, S_ref, br, bk, bv, bsm, *, tS):
    C = CHUNK
    P = 2 * C_DIM
    NP = C_WIDTH // P
    s = pl.program_id(1)

    @pl.when(s == 0)
    def _():
        S_ref[...] = jnp.zeros_like(S_ref)
        for buf in (br, bk, bv, bsm):
            buf[0:8, :] = jnp.zeros((8, buf.shape[1]), F32)

    def shift(x_ref, buf, mu_ref):
        x = x_ref[...]
        buf[8:8 + tS, :] = x
        prev = pltpu.roll(buf[...], 1, axis=0)[8:8 + tS, :]
        buf[0:8, :] = buf[tS:tS + 8, :]
        return x + (prev - x) * mu_ref[...]

    r = shift(r_ref, br, mur_ref)
    k = shift(k_ref, bk, muk_ref)
    v = shift(v_ref, bv, muv_ref)
    sm = shift(sm_ref, bsm, musm_ref)
    xw, xa, xg = sm[:, 0:LANES], sm[:, SM_XA:SM_XA + LANES], sm[:, SM_XG:SM_XG + C_GATE_LORA]
    lw = -C_DECAY_SCALE * jax.nn.sigmoid(w0_ref[...] + _mm(jnp.tanh(xw), w2_ref[...]))
    a = jax.nn.sigmoid(a0_ref[...] + _mm(xa, a2_ref[...]))
    gate = _mm(jax.nn.sigmoid(xg), g2_ref[...])
    kkraw = k * kkw_ref[...]
    k = k * (1.0 + (a - 1.0) * kaw_ref[...])

    n_c = tS // C
    nb = NP * n_c
    ng = nb * 2 * C // GROUP
    m0 = lax.broadcasted_iota(jnp.int32, (1, P), 1) < C_DIM
    rs = lax.broadcasted_iota(jnp.int32, (tS, tS), 0)
    cs = lax.broadcasted_iota(jnp.int32, (tS, tS), 1)
    tril_b = jnp.where((rs >= cs) & ((rs ^ cs) < C), 1.0, 0.0).astype(BF16)
    r2 = lax.broadcasted_iota(jnp.int32, (1, 2 * C, 2 * C), 1)
    c2 = lax.broadcasted_iota(jnp.int32, (1, 2 * C, 2 * C), 2)
    strict2 = r2 > c2
    rt = lax.broadcasted_iota(jnp.int32, (1, C, 2 * C), 1)
    ct = lax.broadcasted_iota(jnp.int32, (1, C, 2 * C), 2)
    incl = rt >= (ct & (C - 1))

    def seg_sum(x):
        s0 = jnp.sum(jnp.where(m0, x, 0.0), axis=-1, keepdims=True)
        s1 = jnp.sum(jnp.where(m0, 0.0, x), axis=-1, keepdims=True)
        return jnp.where(m0, s0, s1)

    def pairs(x):
        return jnp.stack([x[:, p * P:(p + 1) * P] for p in range(NP)], axis=0).reshape(nb, C, P)

    def stack(x):
        return jnp.concatenate([jnp.where(m0, x, 0.0), jnp.where(m0, 0.0, x)], axis=1)

    lw_hi = lw.astype(BF16)
    lw_lo = (lw - lw_hi.astype(F32)).astype(BF16)
    cum2 = jnp.dot(tril_b, jnp.concatenate([lw_hi, lw_lo], axis=1), preferred_element_type=F32)
    cum = cum2[:, :C_WIDTH] + cum2[:, C_WIDTH:]
    kkraw = pairs(kkraw)
    kk = kkraw * lax.rsqrt(seg_sum(kkraw * kkraw) + L2_EPS)
    bv = kk * pairs(a)
    w_inc = pairs(jnp.exp(cum))
    w_inv = pairs(jnp.exp(-cum))
    w_end = w_inc[:, C - 1:C, :]
    r_t = pairs(r) * w_inc
    a_st = stack(-kk * pairs(jnp.exp(cum - lw)))
    b_st = stack(bv * w_inv)
    k_st = stack(pairs(k) * w_inv)
    v_st = stack(pairs(v))
    aa = _bmm_nt(jnp.concatenate([a_st, r_t], axis=1), jnp.concatenate([b_st, k_st], axis=1))
    a_ab = jnp.where(strict2, aa[:, :2 * C, :2 * C], 0.0)
    a_ak = jnp.where(strict2, aa[:, :2 * C, 2 * C:], 0.0)
    a_rb = jnp.where(incl, aa[:, 2 * C:, :2 * C], 0.0)
    a_rk = jnp.where(incl, aa[:, 2 * C:, 2 * C:], 0.0)
    a4 = a_ab.reshape(ng, 2, 2 * C, 2 * C)
    a_cat = jnp.concatenate([a4[:, 0, :C] + a4[:, 0, C:], a4[:, 1, :C] + a4[:, 1, C:]], axis=2)
    tinv = _neumann_inverse_bd(a_cat, _group_masks(ng)[0])
    x = jnp.concatenate([a_st, _bmm(a_ak, v_st)], axis=2)
    wu = _bmm(tinv, x.reshape(ng, GROUP, 2 * P)).reshape(nb, 2 * C, 2 * P)
    rw = _bmm(a_rb, wu)
    r_eff = r_t + rw[:, :, :P]
    o0 = rw[:, :, P:] + _bmm(a_rk, v_st)
    mz = _bmm_tn(wu, b_st * w_end)
    z = mz[:, P:, :] + _bmm_tn(v_st, k_st * w_end)
    S = [S_ref[p] for p in range(NP)]
    outs = [[None] * n_c for _ in range(NP)]
    for c in range(n_c):
        for p in range(NP):
            i = p * n_c + c
            outs[p][c] = _mm_nt(r_eff[i], S[p]) + o0[i]
            S[p] = S[p] * w_end[i] + _mm(S[p], mz[i, :P, :]) + z[i]
    for p in range(NP):
        S_ref[p] = S[p]
        ps = slice(p * P, (p + 1) * P)
        o = jnp.concatenate(outs[p], axis=0)
        mean = seg_sum(o) * (1.0 / C_DIM)
        d = o - mean
        var = seg_sum(d * d) * (1.0 / C_DIM)
        on = d * lax.rsqrt(var + C_GN_EPS) * gnw_ref[:, ps] + gnb_ref[:, ps]
        bonus = seg_sum(r[:, ps] * k[:, ps] * rk_ref[:, ps]) * v[:, ps]
        o_ref[:, ps] = ((on + bonus) * gate[:, ps]).astype(o_ref.dtype)


def _rwkv(cols, mu_r, mu_k, mu_v, mu_sm, w0, w2p, a0, a2p, g2, k_k, k_a, r_k, gn_w, gn_b, Bn, S, tS=256):
    T = cols.shape[0]
    nS = S // tS
    W = C_WIDTH
    P = 2 * C_DIM

    def col(off, width):
        return pl.BlockSpec((tS, width), lambda b, s: (b * nS + s, off // width))

    def par(rows, width=W):
        return pl.BlockSpec((rows, width), lambda b, s: (0, 0))

    return pl.pallas_call(
        functools.partial(_rwkv_body, tS=tS),
        grid=(Bn, nS),
        in_specs=[col(P_CR, W), col(P_CK, W), col(P_CV, W), col(P_SM, SM_W), par(1), par(1), par(1),
                  par(1, SM_W), par(1), par(LANES), par(1), par(LANES), par(C_GATE_LORA), par(1), par(1),
                  par(1), par(1), par(1)],
        out_specs=pl.BlockSpec((tS, W), lambda b, s: (b * nS + s, 0)),
        out_shape=jax.ShapeDtypeStruct((T, W), BF16),
        scratch_shapes=[pltpu.VMEM((W // P, P, P), F32)] + [pltpu.VMEM((tS + 8, W), F32)] * 3
                       + [pltpu.VMEM((tS + 8, SM_W), F32)],
        compiler_params=_cparams(("parallel", "arbitrary")),
        name="rwkv",
    )(cols, cols, cols, cols, mu_r, mu_k, mu_v, mu_sm, w0, w2p, a0, a2p, g2, k_k, k_a, r_k, gn_w, gn_b)


def _merge_body(x_ref, ya_ref, yb_ref, yc_ref, g_ref, pa_ref, pb_ref, pc_ref, wo_ref, o_ref):
    D = x_ref.shape[1]
    merged = (jax.nn.sigmoid(g_ref[:, 0:D]) * jnp.dot(ya_ref[...], pa_ref[...], preferred_element_type=F32)
              + jax.nn.sigmoid(g_ref[:, D:2 * D]) * jnp.dot(yb_ref[...], pb_ref[...], preferred_element_type=F32)
              + jax.nn.sigmoid(g_ref[:, 2 * D:]) * jnp.dot(yc_ref[...], pc_ref[...], preferred_element_type=F32))
    o_ref[...] = x_ref[...] + jnp.dot(merged.astype(BF16), wo_ref[...], preferred_element_type=F32)


def _merge(x, y_a, y_b, y_c, cols, proj_a, proj_b, proj_c, w_out, l, tm=256):
    T, D = x.shape

    def rows(width):
        return pl.BlockSpec((tm, width), lambda i: (i, 0))

    def resident(kdim):
        return pl.BlockSpec((None, kdim, D), lambda i: (l, 0, 0), pipeline_mode=pl.Buffered(1))

    return pl.pallas_call(
        _merge_body,
        grid=(T // tm,),
        in_specs=[rows(D), rows(A_OUT), rows(B_WIDTH), rows(C_WIDTH),
                  pl.BlockSpec((tm, 3 * D), lambda i: (i, P_G // (3 * D))),
                  resident(A_OUT), resident(B_WIDTH), resident(C_WIDTH), resident(D)],
        out_specs=rows(D),
        out_shape=jax.ShapeDtypeStruct((T, D), F32),
        compiler_params=_cparams(("parallel",)),
        name="merge",
    )(x, y_a, y_b, y_c, cols, proj_a, proj_b, proj_c, w_out)


def _cast_body(w_ref, o_ref):
    o_ref[...] = w_ref[...].astype(BF16)


def _interleave_body(g_ref, u_ref, o_ref):
    o_ref[...] = jnp.concatenate([g_ref[...], u_ref[...]], axis=1).astype(BF16)


def _interleave_gate_up(w, tf=FFN_TF):
    L, R, C2 = w.shape
    nj = C2 // (2 * tf)
    return pl.pallas_call(
        _interleave_body,
        grid=(L, nj),
        in_specs=[pl.BlockSpec((None, R, tf), lambda l, j: (l, 0, j)),
                  pl.BlockSpec((None, R, tf), lambda l, j: (l, 0, j + nj))],
        out_specs=pl.BlockSpec((None, R, 2 * tf), lambda l, j: (l, 0, j)),
        out_shape=jax.ShapeDtypeStruct(w.shape, BF16),
        compiler_params=_cparams(("parallel", "parallel")),
        name="cast_gate_up",
    )(w, w)


def _to_bf16(w):
    L, R, Cw = w.shape
    tr = 128 if Cw > 4096 else 256
    spec = pl.BlockSpec((None, tr, Cw), lambda l, i: (l, i, 0))
    return pl.pallas_call(
        _cast_body,
        grid=(L, R // tr),
        in_specs=[spec],
        out_specs=spec,
        out_shape=jax.ShapeDtypeStruct(w.shape, BF16),
        compiler_params=_cparams(("parallel", "parallel")),
        name="cast_bf16",
    )(w)


def _pack_body(w_ref, sm_ref, o_ref, *, n_big):
    t = pl.program_id(1)
    o_ref[...] = jnp.where(t >= n_big, sm_ref[...], w_ref[0]).astype(BF16)


def _pack_w_in(w, tr=512):
    L, D, _ = w.shape
    a_cols = 3 * A_HEADS * HEAD_A
    b_ab = a_cols + 4 * B_WIDTH
    c0 = b_ab + 2 * B_HEADS
    c_xw = c0 + 3 * C_WIDTH
    c_xa = c_xw + C_DECAY_LORA
    c_xg = c_xa + C_ICLR_LORA
    g0 = c_xg + C_GATE_LORA
    wt = jnp.swapaxes(w, 1, 2)
    z = lambda n: jnp.zeros((L, n, D), w.dtype)
    small = jnp.concatenate([wt[:, c_xw:c_xa], wt[:, b_ab:c0], z(16), wt[:, c_xa:c_xg], z(32), wt[:, c_xg:g0]],
                            axis=1)
    n_big = P_SM // tr
    n_small = SM_W // tr

    def src_row(t):
        r = t * tr
        off = jnp.where(r < P_BQ, g0 - P_G,
                        jnp.where(r < P_CR, a_cols - P_BQ, jnp.where(r < P_AQ, c0 - P_CR, 0 - P_AQ)))
        return pl.multiple_of(jnp.where(t < n_big, r + off, 0), 16)

    return pl.pallas_call(
        functools.partial(_pack_body, n_big=n_big),
        grid=(L, n_big + n_small),
        in_specs=[pl.BlockSpec((pl.Element(1), pl.Element(tr), pl.Element(D)), lambda l, t: (l, src_row(t), 0)),
                  pl.BlockSpec((None, tr, D), lambda l, t: (l, jnp.clip(t - n_big, 0, n_small - 1), 0))],
        out_specs=pl.BlockSpec((None, tr, D), lambda l, t: (l, t, 0)),
        out_shape=jax.ShapeDtypeStruct((L, P_COLS, D), BF16),
        compiler_params=_cparams(("parallel", "arbitrary")),
        name="pack_w_in",
    )(wt, small)


def _pad_rows(w, n):
    return jnp.concatenate([w, jnp.zeros((n - w.shape[0], w.shape[1]), w.dtype)], axis=0)


def _mixers(x, cols, l, Bn, S, b_conv, b_a_log, b_dt_bias, b_norm, c_mu, c_w0, c_w2, c_a0, c_a2, c_g2,
            c_k_k, c_k_a, c_r_k, c_gn_w, c_gn_b):
    y_a = _attention(cols, Bn, S)
    y_b = _gdn(cols, b_conv[l], b_a_log[l], b_dt_bias[l], b_norm[l], Bn, S)
    W = C_WIDTH
    mu = c_mu[l]
    z = lambda n: jnp.zeros((n,), F32)
    mu_sm = jnp.concatenate([mu[3 * W:3 * W + C_DECAY_LORA], z(32),
                             mu[3 * W + C_DECAY_LORA:3 * W + C_DECAY_LORA + C_ICLR_LORA], z(32),
                             mu[3 * W + C_DECAY_LORA + C_ICLR_LORA:]])
    row = lambda t: t.reshape(1, -1)
    y_c = _rwkv(cols, row(mu[0:W]), row(mu[W:2 * W]), row(mu[2 * W:3 * W]), row(mu_sm),
                row(c_w0[l]), _pad_rows(c_w2[l], LANES).astype(BF16), row(c_a0[l]),
                _pad_rows(c_a2[l], LANES).astype(BF16), c_g2[l].astype(BF16),
                row(c_k_k[l]), row(c_k_a[l]), row(c_r_k[l]), row(c_gn_w[l]), row(c_gn_b[l]), Bn, S)
    return y_a, y_b, y_c


def kernel(x, ffn1_norm, ffn1_w_gu, ffn1_w_down, mix_norm, w_in, b_conv, b_a_log, b_dt_bias, b_norm,
           c_mu, c_w0, c_w2, c_a0, c_a2, c_g2, c_k_k, c_k_a, c_r_k, c_gn_w, c_gn_b,
           proj_a, proj_b, proj_c, w_out, ffn2_norm, ffn2_w_gu, ffn2_w_down, final_norm):
    Bn, S, D = x.shape
    depth = w_in.shape[0]
    x = x.reshape(Bn * S, D)
    f1_gu, f2_gu = _interleave_gate_up(ffn1_w_gu), _interleave_gate_up(ffn2_w_gu)
    f1_dn, f2_dn = _to_bf16(ffn1_w_down), _to_bf16(ffn2_w_down)
    p_a, p_b, p_c, w_o = (_to_bf16(w) for w in (proj_a, proj_b, proj_c, w_out))
    w_packed = _pack_w_in(w_in)
    for l in range(depth):
        x = _ffn(x, ffn1_norm[l], f1_gu, f1_dn, l)
        cols = _inproj(x, mix_norm[l], w_packed, l)
        y_a, y_b, y_c = _mixers(x, cols, l, Bn, S, b_conv, b_a_log, b_dt_bias, b_norm, c_mu, c_w0,
                                c_w2, c_a0, c_a2, c_g2, c_k_k, c_k_a, c_r_k, c_gn_w, c_gn_b)
        x = _merge(x, y_a, y_b, y_c, cols, p_a, p_b, p_c, w_o, l)
        x = _ffn(x, ffn2_norm[l], f2_gu, f2_dn, l, out_norm=final_norm if l == depth - 1 else None)
    return x.reshape(Bn, S, D)
```
